```python
import math
import jax
import jax.numpy as jnp
from jax import lax
import numpy as np

D_MODEL = 1024
BATCH = 1
SEQ = 16384
DEPTH = 2
DEC_BATCH = 128
DEC_SEQ = 8
PAST_LEN = 16384
PAGE_SIZE = 128

N_A_LAYERS = (DEPTH + 1) // 2
N_C_LAYERS = DEPTH // 2
DEEPNORM_ALPHA = (2.0 * DEPTH) ** 0.25
DEEPNORM_BETA = (8.0 * DEPTH) ** -0.25
LN_EPS = 1e-5
RMS_EPS = 1e-6
NEG = -1e30
Q_BLOCK = 128

MLA_HEADS = D_MODEL // 128
Q_RANK = 3 * D_MODEL // 8
KV_RANK = D_MODEL // 4
NOPE_DIM = 64
ROPE_DIM = 32
MLA_V_DIM = 64
MLA_WIDTH = MLA_HEADS * MLA_V_DIM
MLA_SCALE = (NOPE_DIM + ROPE_DIM) ** -0.5
ROPE_BASE = 10000.0

SSM_HEADDIM = 64
D_INNER = D_MODEL
SSM_HEADS = D_INNER // SSM_HEADDIM
SSM_GROUPS = 2
D_STATE = 128
CONV_W = 4
CONV_DIM = D_INNER + 2 * SSM_GROUPS * D_STATE
SSD_CHUNK = 128

NSA_HEADS = D_MODEL // 64
NSA_GROUPS = 2
NSA_HPG = NSA_HEADS // NSA_GROUPS
NSA_DK = 64
NSA_DV = 64
NSA_WIDTH = NSA_HEADS * NSA_DV
NSA_SCALE = NSA_DK ** -0.5
CMP_LEN = 32
CMP_STRIDE = 16
SLC_BLOCK = 64
N_SEL = 16
WINDOW = 512

NUM_BUCKETS = 32
MAX_DISTANCE = 128

A_SPLITS = (Q_RANK, KV_RANK, ROPE_DIM, MLA_WIDTH, D_INNER, CONV_DIM, SSM_HEADS)
IN_A = sum(A_SPLITS)
C_SPLITS = (NSA_HEADS * NSA_DK,
            NSA_GROUPS * NSA_DK, NSA_GROUPS * NSA_DV,
            NSA_GROUPS * NSA_DK, NSA_GROUPS * NSA_DV,
            NSA_GROUPS * NSA_DK, NSA_GROUPS * NSA_DV,
            3 * NSA_HEADS, NSA_WIDTH)
IN_C = sum(C_SPLITS)

kernel_name = 'hybrid_mla_ssd_nsa_decode_step'


def _split(v, sizes):
    return jnp.split(v, [int(c) for c in np.cumsum(sizes)[:-1]], axis=-1)


def _layernorm(x, g, b):
    xf = x.astype(jnp.float32)
    mu = jnp.mean(xf, -1, keepdims=True)
    var = jnp.mean(jnp.square(xf - mu), -1, keepdims=True)
    return ((xf - mu) * lax.rsqrt(var + LN_EPS) * g + b).astype(x.dtype)


def _rmsnorm(x, g):
    xf = x.astype(jnp.float32)
    return (xf * lax.rsqrt(jnp.mean(xf * xf, -1, keepdims=True) + RMS_EPS) * g).astype(x.dtype)


def _masked_softmax(s, mask):
    s = jnp.where(mask, s.astype(jnp.float32), NEG)
    return jax.nn.softmax(s, axis=-1) * mask


def _rope_cs(pos):
    half = ROPE_DIM // 2
    inv = ROPE_BASE ** (-jnp.arange(half, dtype=jnp.float32) / half)
    ang = pos.astype(jnp.float32)[:, None] * inv[None, :]
    return jnp.cos(ang), jnp.sin(ang)


def _rope(x, cos, sin):
    half = ROPE_DIM // 2
    xf = x.astype(jnp.float32)
    x1, x2 = xf[..., :half], xf[..., half:]
    return jnp.concatenate([x1 * cos - x2 * sin, x2 * cos + x1 * sin], -1).astype(x.dtype)


def _t5_bucket(dist):
    n = jnp.maximum(dist, 0)
    exact = NUM_BUCKETS // 2
    nf = jnp.maximum(n, exact).astype(jnp.float32)
    large = exact + (jnp.log(nf / exact) / math.log(MAX_DISTANCE / exact)
                     * (NUM_BUCKETS - exact)).astype(jnp.int32)
    return jnp.where(n < exact, n, jnp.minimum(large, NUM_BUCKETS - 1))


def _bias_qk(tab, dist):
    return jnp.transpose(tab[_t5_bucket(dist)], (2, 3, 0, 1)).astype(jnp.float32)


def _a_front(x, pos, w_in, q_norm, w_uq, kv_norm, w_uk):
    B_, S = x.shape[:2]
    proj = x @ w_in
    c_q, c_kv, k_r, g_mla, z, xbc, dt_raw = _split(proj, A_SPLITS)
    q = (_rmsnorm(c_q, q_norm) @ w_uq).reshape(B_, S, MLA_HEADS, NOPE_DIM + ROPE_DIM)
    cos, sin = _rope_cs(pos)
    q_rope = _rope(q[..., NOPE_DIM:], cos[:, None], sin[:, None])
    q_lat = jnp.einsum('bshd,chd->bshc', q[..., :NOPE_DIM], w_uk)
    lat = _rmsnorm(c_kv, kv_norm)
    k_rope = _rope(k_r, cos, sin)
    return q_lat, q_rope, lat, k_rope, g_mla, z, xbc, dt_raw


def _mla_prompt(q_lat, q_rope, lat, krope):
    B_, S = lat.shape[:2]
    kpos = jnp.arange(S)

    def blk(i):
        st = i * Q_BLOCK
        ql = lax.dynamic_slice_in_dim(q_lat, st, Q_BLOCK, axis=1)
        qr = lax.dynamic_slice_in_dim(q_rope, st, Q_BLOCK, axis=1)
        s = (jnp.einsum('bqhc,bkc->bhqk', ql, lat)
             + jnp.einsum('bqhr,bkr->bhqk', qr, krope)) * MLA_SCALE
        qpos = st + jnp.arange(Q_BLOCK)
        p = _masked_softmax(s, kpos[None, :] <= qpos[:, None])
        return jnp.einsum('bhqk,bkc->bqhc', p, lat)

    o = lax.map(blk, jnp.arange(S // Q_BLOCK))
    return jnp.moveaxis(o, 0, 1).reshape(B_, S, MLA_HEADS, KV_RANK)


def _mla_sample(q_lat, q_rope, lat_new, kr_new, pool_lat, pool_kr, li, page_table):
    DB, S = lat_new.shape[:2]
    ql = q_lat.astype(jnp.float32)
    qr = q_rope.astype(jnp.float32)

    def scores(lat, kr):
        return (jnp.einsum('bqhc,bkc->bhqk', ql, lat.astype(jnp.float32))
                + jnp.einsum('bqhr,bkr->bhqk', qr, kr.astype(jnp.float32))) * MLA_SCALE

    def step(carry, phys):
        m, l, acc = carry
        lat = pool_lat[li, phys]
        kr = pool_kr[li, phys]
        s = scores(lat, kr)
        m_new = jnp.maximum(m, s.max(-1))
        corr = jnp.exp(m - m_new)
        p = jnp.exp(s - m_new[..., None])
        acc = acc * corr[..., None] + jnp.einsum('bhqk,bkc->bhqc', p, lat.astype(jnp.float32))
        return (m_new, l * corr + p.sum(-1), acc), None

    init = (jnp.full((DB, MLA_HEADS, S), NEG, jnp.float32),
            jnp.zeros((DB, MLA_HEADS, S), jnp.float32),
            jnp.zeros((DB, MLA_HEADS, S, KV_RANK), jnp.float32))
    (m, l, acc), _ = lax.scan(step, init, page_table.T)
    causal = jnp.tril(jnp.ones((S, S), bool))
    s = jnp.where(causal, scores(lat_new, kr_new), NEG)
    m_new = jnp.maximum(m, s.max(-1))
    corr = jnp.exp(m - m_new)
    p = jnp.exp(s - m_new[..., None]) * causal
    l = l * corr + p.sum(-1)
    acc = acc * corr[..., None] + jnp.einsum('bhqk,bkc->bhqc', p, lat_new.astype(jnp.float32))
    return jnp.moveaxis(acc / l[..., None], 1, 2)


def _ssd(x, dt, A, Bm, Cm, h0):
    b, L = x.shape[:2]
    Q = SSD_CHUNK if L % SSD_CHUNK == 0 else L
    nc = L // Q
    R = SSM_HEADS // SSM_GROUPS
    xf = x.astype(jnp.float32).reshape(b, nc, Q, SSM_GROUPS, R, SSM_HEADDIM)
    dtc = dt.reshape(b, nc, Q, SSM_GROUPS, R)
    Bc = Bm.astype(jnp.float32).reshape(b, nc, Q, SSM_GROUPS, D_STATE)
    Cc = Cm.astype(jnp.float32).reshape(b, nc, Q, SSM_GROUPS, D_STATE)
    cs = jnp.cumsum(dtc * A.reshape(SSM_GROUPS, R), axis=2)
    csT = jnp.moveaxis(cs, 2, -1)
    tril = jnp.tril(jnp.ones((Q, Q), bool))
    decay = jnp.exp(jnp.where(tril, csT[..., :, None] - csT[..., None, :], -jnp.inf))
    cb = jnp.einsum('bcign,bcjgn->bcgij', Cc, Bc)
    xdt = xf * dtc[..., None]
    y_diag = jnp.einsum('bcgrij,bcjgrp->bcigrp', cb[:, :, :, None] * decay, xdt)
    to_end = jnp.exp(cs[:, :, -1:] - cs)
    states = jnp.einsum('bcjgn,bcjgrp->bcgrpn', Bc, xdt * to_end[..., None])
    chunk_decay = jnp.exp(cs[:, :, -1])

    def step(h, inp):
        st, dec = inp
        return h * dec[..., None, None] + st, h

    h_last, h_prev = lax.scan(step,
                              h0.astype(jnp.float32).reshape(b, SSM_GROUPS, R, SSM_HEADDIM, D_STATE),
                              (jnp.moveaxis(states, 1, 0), jnp.moveaxis(chunk_decay, 1, 0)))
    h_prev = jnp.moveaxis(h_prev, 0, 1)
    y_off = jnp.einsum('bcign,bcgrpn->bcigrp', Cc, h_prev) * jnp.exp(cs)[..., None]
    y = (y_diag + y_off).reshape(b, L, SSM_HEADS, SSM_HEADDIM)
    return y, h_last.reshape(b, SSM_HEADS, SSM_HEADDIM, D_STATE)


def _mamba(xbc, dt_raw, conv_buf, h0, conv_w, conv_b, dt_bias, a_log, d_skip):
    B_, S = xbc.shape[:2]
    xpad = jnp.concatenate([conv_buf.astype(xbc.dtype), xbc], 1)
    conv = lax.conv_general_dilated(xpad, conv_w[:, None, :].astype(xbc.dtype), (1,), 'VALID',
                                    dimension_numbers=('NWC', 'WIO', 'NWC'),
                                    feature_group_count=CONV_DIM) + conv_b
    new_buf = xpad[:, S:]
    xs, Bm, Cm = _split(jax.nn.silu(conv), (D_INNER, SSM_GROUPS * D_STATE, SSM_GROUPS * D_STATE))
    dt = jax.nn.softplus((dt_raw + dt_bias).astype(jnp.float32))
    A = -jnp.exp(a_log.astype(jnp.float32))
    xh = xs.reshape(B_, S, SSM_HEADS, SSM_HEADDIM)
    y, h = _ssd(xh, dt, A, Bm.reshape(B_, S, SSM_GROUPS, D_STATE),
                Cm.reshape(B_, S, SSM_GROUPS, D_STATE), h0)
    y = y + xh.astype(jnp.float32) * d_skip.astype(jnp.float32)[:, None]
    return y.reshape(B_, S, D_INNER).astype(xbc.dtype), new_buf, h


def _a_back(x, o_lat, g_mla, y_ssm, z, w_uv, ssm_norm, w_out, ln_g, ln_b):
    B_, S = x.shape[:2]
    o_mla = jnp.einsum('bshc,che->bshe', o_lat.astype(x.dtype), w_uv).reshape(B_, S, MLA_WIDTH)
    o_mla = o_mla * jax.nn.silu(g_mla)
    y = _rmsnorm(y_ssm * jax.nn.silu(z), ssm_norm)
    out = jnp.concatenate([o_mla, y], -1) @ w_out
    return _layernorm(DEEPNORM_ALPHA * x + out, ln_g, ln_b)


def _c_front(x, w_in):
    B_, S = x.shape[:2]
    q, kc, vc, ks, vs, kw, vw, gts, og = _split(x @ w_in, C_SPLITS)
    r = lambda t: t.reshape(B_, S, NSA_GROUPS, -1)
    gates = jax.nn.sigmoid(gts.astype(jnp.float32)).reshape(B_, S, 3, NSA_HEADS)
    return q.reshape(B_, S, NSA_HEADS, NSA_DK), r(kc), r(vc), r(ks), r(vs), r(kw), r(vw), gates, og


def _compress(k, pe, w1, w2):
    B_, L, G, d = k.shape
    kk = jnp.moveaxis(k, 2, 1).reshape(B_ * G, L, d)
    h = lax.conv_general_dilated(kk, w1.astype(kk.dtype), (CMP_STRIDE,), 'VALID',
                                 dimension_numbers=('NWC', 'WIO', 'NWC'))
    h = h + jnp.einsum('ld,ldh->h', pe, w1)
    out = jax.nn.silu(h) @ w2
    n = out.shape[1]
    return jnp.moveaxis(out.reshape(B_, G, n, d), 1, 2)


def _slc_overlap(n_cmp, n_slc):
    cs = jnp.arange(n_cmp)[:, None] * CMP_STRIDE
    bs = jnp.arange(n_slc)[None, :] * SLC_BLOCK
    ov = jnp.minimum(cs + CMP_LEN, bs + SLC_BLOCK) - jnp.maximum(cs, bs)
    return jnp.maximum(ov, 0).astype(jnp.float32)


def _nsa_block(q, gates, tpos, kc, vc, cmp_end, ov, wk, wv, wpos, gather, rel_bias):
    B_, Qn = q.shape[:2]
    qg = q.reshape(B_, Qn, NSA_GROUPS, NSA_HPG, NSA_DK)
    tab = rel_bias.reshape(NUM_BUCKETS, NSA_GROUPS, NSA_HPG)
    dist_c = tpos[:, None] - cmp_end[None, :]
    s_c = jnp.einsum('bqghd,bkgd->bghqk', qg, kc) * NSA_SCALE + _bias_qk(tab, dist_c)
    p_c = _masked_softmax(s_c, dist_c >= 0)
    o_c = jnp.einsum('bghqk,bkgd->bqghd', p_c, vc)
    n_slc = ov.shape[1]
    imp = jnp.einsum('bghqk,kn->bgqn', p_c, ov)
    blk = jnp.arange(n_slc)[None, :]
    cur = (tpos // SLC_BLOCK)[:, None]
    avail = blk * SLC_BLOCK <= tpos[:, None]
    forced = (blk == 0) | (blk == cur) | (blk == cur - 1)
    score = jnp.where(avail, jnp.where(forced, jnp.inf, imp), -jnp.inf)
    _, sel = lax.top_k(score, min(N_SEL, n_slc))
    pos = (sel[..., None] * SLC_BLOCK + jnp.arange(SLC_BLOCK)).reshape(B_, NSA_GROUPS, Qn, -1)
    k_s, v_s = gather(pos)
    dist_s = tpos[None, None, :, None] - pos
    bias_s = jnp.moveaxis(tab[_t5_bucket(dist_s), jnp.arange(NSA_GROUPS)[None, :, None, None]], -1, 2)
    s_s = jnp.einsum('bqghd,bgqkd->bghqk', qg, k_s) * NSA_SCALE + bias_s.astype(jnp.float32)
    p_s = _masked_softmax(s_s, (dist_s >= 0)[:, :, None])
    o_s = jnp.einsum('bghqk,bgqkd->bqghd', p_s, v_s)
    dist_w = tpos[:, None] - wpos[None, :]
    mask_w = (dist_w >= 0) & (dist_w < WINDOW) & (wpos >= 0)[None, :]
    s_w = jnp.einsum('bqghd,bkgd->bghqk', qg, wk) * NSA_SCALE + _bias_qk(tab, dist_w)
    p_w = _masked_softmax(s_w, mask_w)
    o_w = jnp.einsum('bghqk,bkgd->bqghd', p_w, wv)
    shp = (B_, Qn, NSA_HEADS, NSA_DV)
    return (gates[:, :, 0, :, None] * o_c.reshape(shp) + gates[:, :, 1, :, None] * o_s.reshape(shp)
            + gates[:, :, 2, :, None] * o_w.reshape(shp))


def _nsa_prompt(q, gates, kc_r, vc_r, ks, vs, kw, vw, pe_k, w1_k, w2_k, pe_v, w1_v, w2_v, rel_bias):
    B_, L = q.shape[:2]
    kc = _compress(kc_r, pe_k, w1_k, w2_k)
    vc = _compress(vc_r, pe_v, w1_v, w2_v)
    n_cmp = kc.shape[1]
    cmp_end = jnp.arange(n_cmp) * CMP_STRIDE + (CMP_LEN - 1)
    ov = _slc_overlap(n_cmp, -(-L // SLC_BLOCK))
    bi = jnp.arange(B_)[:, None, None, None]
    gi = jnp.arange(NSA_GROUPS)[None, :, None, None]

    def gather(pos):
        p = jnp.minimum(pos, L - 1)
        return ks[bi, p, gi], vs[bi, p, gi]

    pad = ((0, 0), (WINDOW, 0), (0, 0), (0, 0))
    kwp, vwp = jnp.pad(kw, pad), jnp.pad(vw, pad)

    def blk(i):
        st = i * Q_BLOCK
        tpos = st + jnp.arange(Q_BLOCK)
        wpos = st - WINDOW + jnp.arange(WINDOW + Q_BLOCK)
        sl = lambda a, n: lax.dynamic_slice_in_dim(a, st, n, axis=1)
        return _nsa_block(sl(q, Q_BLOCK), sl(gates, Q_BLOCK), tpos, kc, vc, cmp_end, ov,
                          sl(kwp, WINDOW + Q_BLOCK), sl(vwp, WINDOW + Q_BLOCK), wpos, gather, rel_bias)

    o = lax.map(blk, jnp.arange(L // Q_BLOCK))
    return jnp.moveaxis(o, 0, 1).reshape(B_, L, NSA_HEADS, NSA_DV)


def _paged_rows(pool, li, page_table, pos, new_rows, past):
    DB, G = pos.shape[:2]
    bi = jnp.arange(DB)[:, None, None, None]
    gi = jnp.arange(G)[None, :, None, None]
    pc = jnp.minimum(pos, past - 1)
    phys = page_table[bi, pc // PAGE_SIZE]
    old = pool[li, phys, pc % PAGE_SIZE, gi]
    pn = jnp.clip(pos - past, 0, new_rows.shape[1] - 1)
    new = new_rows[bi, pn, gi]
    return jnp.where((pos < past)[..., None], old.astype(new.dtype), new)


def _nsa_sample(q, gates, kc_n, vc_n, ks_n, vs_n, kw_n, vw_n, cache_cmp_k, cache_cmp_v,
                cache_slc_k, cache_slc_v, li, page_table, wk_buf, wv_buf,
                pe_k, w1_k, w2_k, pe_v, w1_v, w2_v, rel_bias):
    DB, S = q.shape[:2]
    past = page_table.shape[1] * PAGE_SIZE
    full = lambda pool, new: jnp.concatenate(
        [pool[li, page_table].reshape(DB, past, NSA_GROUPS, -1).astype(new.dtype), new], 1)
    kc = _compress(full(cache_cmp_k, kc_n), pe_k, w1_k, w2_k)
    vc = _compress(full(cache_cmp_v, vc_n), pe_v, w1_v, w2_v)
    n_cmp = kc.shape[1]
    cmp_end = jnp.arange(n_cmp) * CMP_STRIDE + (CMP_LEN - 1)
    ov = _slc_overlap(n_cmp, -(-(past + S) // SLC_BLOCK))
    w_buf = wk_buf.shape[1]
    wk = jnp.concatenate([wk_buf.astype(kw_n.dtype), kw_n], 1)
    wv = jnp.concatenate([wv_buf.astype(vw_n.dtype), vw_n], 1)
    wpos = past - w_buf + jnp.arange(w_buf + S)

    def gather(pos):
        return (_paged_rows(cache_slc_k, li, page_table, pos, ks_n, past),
                _paged_rows(cache_slc_v, li, page_table, pos, vs_n, past))

    def one(i):
        qi = lax.dynamic_slice_in_dim(q, i, 1, axis=1)
        gi = lax.dynamic_slice_in_dim(gates, i, 1, axis=1)
        tpos = jnp.reshape(past + i, (1,))
        return _nsa_block(qi, gi, tpos, kc, vc, cmp_end, ov, wk, wv, wpos, gather, rel_bias)[:, 0]

    o = jnp.moveaxis(lax.map(one, jnp.arange(S)), 0, 1)
    return o, wk[:, S:], wv[:, S:]


def _c_back(x, o, og, w_out, ln_g, ln_b):
    B_, S = x.shape[:2]
    out = (o.reshape(B_, S, NSA_WIDTH).astype(x.dtype) * jax.nn.silu(og)) @ w_out
    return _layernorm(DEEPNORM_ALPHA * x + out, ln_g, ln_b)


def setup_inputs(seed: int = 0) -> dict:
    key = jax.random.key(seed)
    kit = iter(jax.random.split(key, 64))
    nrm = lambda shape, scale: jax.random.normal(next(kit), shape, jnp.float32) * scale
    gain = lambda shape: 1.0 + nrm(shape, 0.05)
    NA, NC = N_A_LAYERS, N_C_LAYERS
    n_pages = PAST_LEN // PAGE_SIZE
    n_use = DEC_BATCH * n_pages
    n_pool = n_use + max(1, n_use // 4)
    w_buf = min(WINDOW, PAST_LEN)
    page_table = jax.random.permutation(next(kit), n_pool)[:n_use].reshape(DEC_BATCH, n_pages).astype(jnp.int32)
    dt0 = jnp.exp(jax.random.uniform(next(kit), (NA, SSM_HEADS), jnp.float32,
                                     math.log(1e-3), math.log(1e-1)))
    inp = {}
    inp['x_prompt'] = nrm((BATCH, SEQ, D_MODEL), 1.0)
    inp['x_sample'] = nrm((DEC_BATCH, DEC_SEQ, D_MODEL), 1.0)
    inp['cache_mla_latent'] = nrm((NA, n_pool, PAGE_SIZE, KV_RANK), 1.0)
    inp['cache_mla_krope'] = nrm((NA, n_pool, PAGE_SIZE, ROPE_DIM), 1.0)
    inp['state_ssm'] = nrm((NA, DEC_BATCH, SSM_HEADS, SSM_HEADDIM, D_STATE), 0.1)
    inp['state_conv'] = nrm((NA, DEC_BATCH, CONV_W - 1, CONV_DIM), 1.0)
    inp['cache_cmp_k'] = nrm((NC, n_pool, PAGE_SIZE, NSA_GROUPS, NSA_DK), 1.0)
    inp['cache_cmp_v'] = nrm((NC, n_pool, PAGE_SIZE, NSA_GROUPS, NSA_DV), 1.0)
    inp['cache_slc_k'] = nrm((NC, n_pool, PAGE_SIZE, NSA_GROUPS, NSA_DK), 1.0)
    inp['cache_slc_v'] = nrm((NC, n_pool, PAGE_SIZE, NSA_GROUPS, NSA_DV), 1.0)
    inp['state_win_k'] = nrm((NC, DEC_BATCH, w_buf, NSA_GROUPS, NSA_DK), 1.0)
    inp['state_win_v'] = nrm((NC, DEC_BATCH, w_buf, NSA_GROUPS, NSA_DV), 1.0)
    inp['page_table'] = page_table
    inp['rel_bias'] = nrm((NUM_BUCKETS, NSA_HEADS), 0.5)
    inp['w_in_a'] = nrm((NA, D_MODEL, IN_A), D_MODEL ** -0.5)
    inp['q_norm'] = gain((NA, Q_RANK))
    inp['w_uq'] = nrm((NA, Q_RANK, MLA_HEADS * (NOPE_DIM + ROPE_DIM)), Q_RANK ** -0.5)
    inp['kv_norm'] = gain((NA, KV_RANK))
    inp['w_uk'] = nrm((NA, KV_RANK, MLA_HEADS, NOPE_DIM), KV_RANK ** -0.5)
    inp['w_uv'] = nrm((NA, KV_RANK, MLA_HEADS, MLA_V_DIM), KV_RANK ** -0.5)
    inp['conv_w'] = nrm((NA, CONV_W, CONV_DIM), CONV_W ** -0.5)
    inp['conv_b'] = nrm((NA, CONV_DIM), 0.02)
    inp['dt_bias'] = dt0 + jnp.log(-jnp.expm1(-dt0))
    inp['a_log'] = jnp.log(jax.random.uniform(next(kit), (NA, SSM_HEADS), jnp.float32, 1.0, 16.0))
    inp['d_skip'] = gain((NA, SSM_HEADS))
    inp['ssm_norm'] = gain((NA, D_INNER))
    inp['w_out_a'] = nrm((NA, MLA_WIDTH + D_INNER, D_MODEL), DEEPNORM_BETA * (MLA_WIDTH + D_INNER) ** -0.5)
    inp['ln_a_g'] = gain((NA, D_MODEL))
    inp['ln_a_b'] = nrm((NA, D_MODEL), 0.02)
    inp['w_in_c'] = nrm((NC, D_MODEL, IN_C), D_MODEL ** -0.5)
    inp['cmp_pe_k'] = nrm((NC, CMP_LEN, NSA_DK), 0.1)
    inp['cmp_w1_k'] = nrm((NC, CMP_LEN, NSA_DK, NSA_DK), (CMP_LEN * NSA_DK) ** -0.5)
    inp['cmp_w2_k'] = nrm((NC, NSA_DK, NSA_DK), NSA_DK ** -0.5)
    inp['cmp_pe_v'] = nrm((NC, CMP_LEN, NSA_DV), 0.1)
    inp['cmp_w1_v'] = nrm((NC, CMP_LEN, NSA_DV, NSA_DV), (CMP_LEN * NSA_DV) ** -0.5)
    inp['cmp_w2_v'] = nrm((NC, NSA_DV, NSA_DV), NSA_DV ** -0.5)
    inp['w_out_c'] = nrm((NC, NSA_WIDTH, D_MODEL), DEEPNORM_BETA * NSA_WIDTH ** -0.5)
    inp['ln_c_g'] = gain((NC, D_MODEL))
    inp['ln_c_b'] = nrm((NC, D_MODEL), 0.02)
    return inp


def reference(x_prompt, x_sample, cache_mla_latent, cache_mla_krope, state_ssm, state_conv,
              cache_cmp_k, cache_cmp_v, cache_slc_k, cache_slc_v, state_win_k, state_win_v,
              page_table, rel_bias, w_in_a, q_norm, w_uq, kv_norm, w_uk, w_uv, conv_w, conv_b,
              dt_bias, a_log, d_skip, ssm_norm, w_out_a, ln_a_g, ln_a_b, w_in_c,
              cmp_pe_k, cmp_w1_k, cmp_w2_k, cmp_pe_v, cmp_w1_v, cmp_w2_v, w_out_c, ln_c_g, ln_c_b):
    Bp, Sp = x_prompt.shape[:2]
    DB, Ss = x_sample.shape[:2]
    past = page_table.shape[1] * PAGE_SIZE
    pos_p = jnp.arange(Sp)
    pos_s = past + jnp.arange(Ss)
    names = ['p_lat', 'p_krope', 'p_ssm', 'p_conv', 'p_cmp_k', 'p_cmp_v', 'p_slc_k', 'p_slc_v',
             'p_win_k', 'p_win_v', 's_lat', 's_krope', 's_ssm', 's_conv', 's_cmp_k', 's_cmp_v',
             's_slc_k', 's_slc_v', 's_win_k', 's_win_v']
    st = {n: [] for n in names}
    hp, hs = x_prompt, x_sample
    for layer in range(DEPTH):
        i = layer // 2
        if layer % 2 == 0:
            front = (w_in_a[i], q_norm[i], w_uq[i], kv_norm[i], w_uk[i])
            mam = (conv_w[i], conv_b[i], dt_bias[i], a_log[i], d_skip[i])
            back = (w_uv[i], ssm_norm[i], w_out_a[i], ln_a_g[i], ln_a_b[i])
            ql, qr, lat, kr, g, z, xbc, dtr = _a_front(hp, pos_p, *front)
            o_lat = _mla_prompt(ql, qr, lat, kr)
            y, cbuf, h = _mamba(xbc, dtr, jnp.zeros((Bp, CONV_W - 1, CONV_DIM), hp.dtype),
                                jnp.zeros((Bp, SSM_HEADS, SSM_HEADDIM, D_STATE), jnp.float32), *mam)
            hp = _a_back(hp, o_lat, g, y, z, *back)
            st['p_lat'].append(lat)
            st['p_krope'].append(kr)
            st['p_ssm'].append(h.astype(x_prompt.dtype))
            st['p_conv'].append(cbuf)
            ql, qr, lat, kr, g, z, xbc, dtr = _a_front(hs, pos_s, *front)
            o_lat = _mla_sample(ql, qr, lat, kr, cache_mla_latent, cache_mla_krope, i, page_table)
            y, cbuf, h = _mamba(xbc, dtr, state_conv[i], state_ssm[i], *mam)
            hs = _a_back(hs, o_lat, g, y, z, *back)
            st['s_lat'].append(lat)
            st['s_krope'].append(kr)
            st['s_ssm'].append(h.astype(state_ssm.dtype))
            st['s_conv'].append(cbuf)
        else:
            cmp = (cmp_pe_k[i], cmp_w1_k[i], cmp_w2_k[i], cmp_pe_v[i], cmp_w1_v[i], cmp_w2_v[i])
            q, kc, vc, ks, vs, kw, vw, gts, og = _c_front(hp, w_in_c[i])
            o = _nsa_prompt(q, gts, kc, vc, ks, vs, kw, vw, *cmp, rel_bias)
            hp = _c_back(hp, o, og, w_out_c[i], ln_c_g[i], ln_c_b[i])
            keep = min(WINDOW, Sp)
            st['p_cmp_k'].append(kc)
            st['p_cmp_v'].append(vc)
            st['p_slc_k'].append(ks)
            st['p_slc_v'].append(vs)
            st['p_win_k'].append(kw[:, Sp - keep:])
            st['p_win_v'].append(vw[:, Sp - keep:])
            q, kc, vc, ks, vs, kw, vw, gts, og = _c_front(hs, w_in_c[i])
            o, nwk, nwv = _nsa_sample(q, gts, kc, vc, ks, vs, kw, vw, cache_cmp_k, cache_cmp_v,
                                      cache_slc_k, cache_slc_v, i, page_table,
                                      state_win_k[i], state_win_v[i], *cmp, rel_bias)
            hs = _c_back(hs, o, og, w_out_c[i], ln_c_g[i], ln_c_b[i])
            st['s_cmp_k'].append(kc)
            st['s_cmp_v'].append(vc)
            st['s_slc_k'].append(ks)
            st['s_slc_v'].append(vs)
            st['s_win_k'].append(nwk)
            st['s_win_v'].append(nwv)
    return (hp, hs,
            jnp.stack(st['p_lat']), jnp.stack(st['p_krope']), jnp.stack(st['p_ssm']), jnp.stack(st['p_conv']),
            jnp.stack(st['p_cmp_k']), jnp.stack(st['p_cmp_v']), jnp.stack(st['p_slc_k']), jnp.stack(st['p_slc_v']),
            jnp.stack(st['p_win_k']), jnp.stack(st['p_win_v']),
            jnp.stack(st['s_lat']), jnp.stack(st['s_krope']), jnp.stack(st['s_ssm']), jnp.stack(st['s_conv']),
            jnp.stack(st['s_cmp_k']), jnp.stack(st['s_cmp_v']), jnp.stack(st['s_slc_k']), jnp.stack(st['s_slc_v']),
            jnp.stack(st['s_win_k']), jnp.stack(st['s_win_v']))
```

```python
import functools
import math

import numpy as np
import jax
import jax.numpy as jnp
from jax import lax
from jax.experimental import pallas as pl
from jax.experimental.pallas import tpu as pltpu

F32 = jnp.float32
BF16 = jnp.bfloat16

D_MODEL = 1024
DEPTH = 2
DEEPNORM_ALPHA = (2.0 * DEPTH) ** 0.25
LN_EPS = 1e-5
RMS_EPS = 1e-6
NEG = -1e30
PAGE = 128

MLA_HEADS = 8
Q_RANK = 384
KV_RANK = 256
NOPE = 64
ROPE = 32
MLA_V = 64
MLA_WIDTH = MLA_HEADS * MLA_V
MLA_SCALE = (NOPE + ROPE) ** -0.5
ROPE_BASE = 10000.0

SSM_HEADDIM = 64
D_INNER = 1024
SSM_HEADS = 16
SSM_GROUPS = 2
D_STATE = 128
CONV_W = 4
CONV_DIM = D_INNER + 2 * SSM_GROUPS * D_STATE
SSD_CHUNK = 128

NSA_HEADS = 16
NSA_GROUPS = 2
NSA_HPG = 8
NSA_D = 64
NSA_SCALE = NSA_D ** -0.5
CMP_LEN = 32
CMP_STRIDE = 16
SLC_BLOCK = 64
N_SEL = 16
WINDOW = 512
NUM_BUCKETS = 32
MAX_DISTANCE = 128
FAR_DIST = 128

TQ = 128
VMEM_LIMIT = 56 * 1024 * 1024


def _cparams(sem):
    return pltpu.CompilerParams(dimension_semantics=sem, vmem_limit_bytes=VMEM_LIMIT)


def _const_spec(shape):
    nd = len(shape)
    return pl.BlockSpec(shape, lambda *a, _nd=nd: (0,) * _nd)


def _dot(a, b):
    return jnp.dot(a, b, preferred_element_type=F32)


def _dot_nt(a, b):
    return lax.dot_general(a, b, (((1,), (1,)), ((), ())), preferred_element_type=F32)


def _dot_tn(a, b):
    return lax.dot_general(a, b, (((0,), (0,)), ((), ())), preferred_element_type=F32)


def _sigmoid(x):
    return 1.0 / (1.0 + jnp.exp(-x))


def _silu(x):
    return x * _sigmoid(x)


def _rms(x, g):
    return x * lax.rsqrt(jnp.mean(x * x, axis=-1, keepdims=True) + RMS_EPS) * g


def _deepnorm_ln(x, out, g, b):
    h = DEEPNORM_ALPHA * x + out
    mu = jnp.mean(h, axis=-1, keepdims=True)
    d = h - mu
    var = jnp.mean(d * d, axis=-1, keepdims=True)
    return d * lax.rsqrt(var + LN_EPS) * g + b


def _t5_bucket_np(dist):
    n = np.maximum(dist, 0)
    exact = NUM_BUCKETS // 2
    nf = np.maximum(n, exact).astype(np.float32)
    large = exact + (np.log(nf / np.float32(exact)) / np.float32(math.log(MAX_DISTANCE / exact))
                     * np.float32(NUM_BUCKETS - exact)).astype(np.int32)
    return np.where(n < exact, n, np.minimum(large, NUM_BUCKETS - 1)).astype(np.int32)


def _row_tile(rows):
    return 256 if rows % 256 == 0 else rows


def _tok_call(body, rows, row_ins, const_ins, outs):
    tm = _row_tile(rows)
    in_specs = [pl.BlockSpec((tm, a.shape[1]), lambda i: (i, 0)) for a in row_ins]
    in_specs += [_const_spec(a.shape) for a in const_ins]
    out_specs = [pl.BlockSpec((tm, w), lambda i: (i, 0)) for w, _ in outs]
    out_shape = [jax.ShapeDtypeStruct((rows, w), dt) for w, dt in outs]
    return pl.pallas_call(
        body, grid=(rows // tm,), in_specs=in_specs, out_specs=out_specs, out_shape=out_shape,
        compiler_params=_cparams(("parallel",)))(*row_ins, *const_ins)


A_CQ, A_CKV, A_G, A_Z, A_XBC, A_MISC, A_END = 0, 384, 640, 1152, 2176, 3712, 3840


def _a_front_body(x_ref, ccq_ref, ssq_ref, cck_ref, ssk_ref, w_ref, qn_ref, kvn_ref, wuq_ref, wuk_ref, dtb_ref,
                  qlat_ref, qrope_ref, lat_ref, latb_ref, kr_ref, krb_ref, g_ref, z_ref, xbc_ref, dt_ref):
    xb = x_ref[...].astype(BF16)

    def proj(lo, hi):
        return _dot(xb, w_ref[:, lo:hi])

    cqn = _rms(proj(A_CQ, A_CKV), qn_ref[...])
    qall = _dot(cqn.astype(BF16), wuq_ref[...])
    qrope = (qall[:, 512:768] * ccq_ref[...] + qall[:, 768:1024] * ssq_ref[...]) * MLA_SCALE
    qrope_ref[...] = qrope.astype(BF16)
    qlat = _dot(qall[:, :512].astype(BF16), wuk_ref[...]) * MLA_SCALE
    qlat_ref[...] = qlat.astype(BF16)
    lat = _rms(proj(A_CKV, A_G), kvn_ref[...])
    lat_ref[...] = lat
    latb_ref[...] = lat.astype(BF16)
    g_ref[...] = proj(A_G, A_Z)
    z_ref[...] = proj(A_Z, A_XBC)
    xbc_ref[...] = proj(A_XBC, A_MISC)
    misc = proj(A_MISC, A_END)
    kr = misc[:, 0:32] * cck_ref[...] + misc[:, 32:64] * ssk_ref[...]
    kr_ref[...] = kr
    krb_ref[...] = kr.astype(BF16)
    v = misc[:, 64:80] + dtb_ref[...]
    dt_ref[...] = jnp.maximum(v, 0.0) + jnp.log1p(jnp.exp(-jnp.abs(v)))


def _rot_cols(w):
    h = ROPE // 2
    return jnp.concatenate([-w[..., h:], w[..., :h]], axis=-1)


def _prep_a(w_in, q_norm, w_uq, kv_norm, w_uk, w_uv, dt_bias, w_out):
    cq, ckv, kr, g, z, xbc, dtw = jnp.split(w_in, np.cumsum([Q_RANK, KV_RANK, ROPE, MLA_WIDTH, D_INNER, CONV_DIM])[:].tolist(),
                                            axis=1)
    pad = jnp.zeros((w_in.shape[0], A_END - A_MISC - 2 * ROPE - SSM_HEADS), w_in.dtype)
    w1 = jnp.concatenate([cq, ckv, g, z, xbc, kr, _rot_cols(kr), dtw, pad], axis=1).astype(BF16)
    wq = w_uq.reshape(Q_RANK, MLA_HEADS, NOPE + ROPE)
    wq_n = wq[:, :, :NOPE].reshape(Q_RANK, MLA_HEADS * NOPE)
    wq_r = wq[:, :, NOPE:]
    wuq = jnp.concatenate([wq_n, wq_r.reshape(Q_RANK, -1), _rot_cols(wq_r).reshape(Q_RANK, -1)], axis=1).astype(BF16)
    eye = jnp.eye(MLA_HEADS, dtype=w_uk.dtype)
    wuk = (jnp.transpose(w_uk, (1, 2, 0))[:, :, None, :] * eye[:, None, :, None]).reshape(
        MLA_HEADS * NOPE, MLA_HEADS * KV_RANK).astype(BF16)
    wuv = (jnp.transpose(w_uv, (1, 0, 2))[:, :, None, :] * eye[:, None, :, None]).reshape(
        MLA_HEADS * KV_RANK, MLA_WIDTH).astype(BF16)
    return dict(w1=w1, qn=q_norm.reshape(1, -1), kvn=kv_norm.reshape(1, -1), wuq=wuq, wuk=wuk, wuv=wuv,
                dtb=dt_bias.reshape(1, -1), wout=w_out.astype(BF16))


def _rope_tables(pos):
    half = ROPE // 2
    inv = ROPE_BASE ** (-jnp.arange(half, dtype=F32) / half)
    ang = pos.astype(F32)[:, None] * inv[None, :]
    c, s = jnp.cos(ang), jnp.sin(ang)
    cck, ssk = jnp.concatenate([c, c], 1), jnp.concatenate([s, s], 1)
    return jnp.tile(cck, (1, MLA_HEADS)), jnp.tile(ssk, (1, MLA_HEADS)), cck, ssk


def _a_front(x, pos, pa):
    rows = x.shape[0]
    ccq, ssq, cck, ssk = _rope_tables(pos)
    outs = [(MLA_HEADS * KV_RANK, BF16), (MLA_HEADS * ROPE, BF16), (KV_RANK, F32), (KV_RANK, BF16), (ROPE, F32),
            (ROPE, BF16), (MLA_WIDTH, F32), (D_INNER, F32), (CONV_DIM, F32), (SSM_HEADS, F32)]
    return _tok_call(_a_front_body, rows, [x, ccq, ssq, cck, ssk],
                     [pa["w1"], pa["qn"], pa["kvn"], pa["wuq"], pa["wuk"], pa["dtb"]], outs)


def _a_back_body(x_ref, ol_ref, g_ref, y_ref, z_ref, wuv_ref, sn_ref, wo_ref, lg_ref, lb_ref, o_ref):
    o_mla = _dot(ol_ref[...], wuv_ref[...])
    a = o_mla * _silu(g_ref[...])
    yn = _rms(y_ref[...] * _silu(z_ref[...]), sn_ref[...])
    out = _dot(a.astype(BF16), wo_ref[:MLA_WIDTH, :]) + _dot(yn.astype(BF16), wo_ref[MLA_WIDTH:, :])
    o_ref[...] = _deepnorm_ln(x_ref[...], out, lg_ref[...], lb_ref[...])


def _a_back(x, o_lat, g, y, z, pa, ssm_norm, ln_g, ln_b):
    (out,) = _tok_call(_a_back_body, x.shape[0], [x, o_lat, g, y, z],
                       [pa["wuv"], ssm_norm.reshape(1, -1), pa["wout"], ln_g.reshape(1, -1), ln_b.reshape(1, -1)],
                       [(D_MODEL, F32)])
    return out


C_Q, C_KV, C_OG, C_END = 0, 1024, 1792, 2816


def _c_front_body(x_ref, w_ref, q_ref, kc_ref, vc_ref, ks_ref, vs_ref, kw_ref, vw_ref, ksb_ref, vsb_ref, kwb_ref,
                  vwb_ref, og_ref):
    xb = x_ref[...].astype(BF16)
    q_ref[...] = _dot(xb, w_ref[:, C_Q:C_KV]) * NSA_SCALE
    kv = _dot(xb, w_ref[:, C_KV:C_OG])
    for j, r in enumerate((kc_ref, vc_ref, ks_ref, vs_ref, kw_ref, vw_ref)):
        r[...] = kv[:, 128 * j:128 * (j + 1)]
    for j, r in enumerate((ksb_ref, vsb_ref, kwb_ref, vwb_ref)):
        r[...] = kv[:, 128 * (j + 2):128 * (j + 3)].astype(BF16)
    og_ref[...] = _dot(xb, w_ref[:, C_OG:C_END])


def _prep_c(w_in, w_out):
    n_kv = 6 * NSA_GROUPS * NSA_D
    w_main = jnp.concatenate([w_in[:, :1024 + n_kv], w_in[:, 1024 + n_kv + 3 * NSA_HEADS:]], axis=1).astype(BF16)
    w_g = w_in[:, 1024 + n_kv:1024 + n_kv + 3 * NSA_HEADS]
    w_grep = jnp.repeat(w_g, NSA_D, axis=1).astype(BF16)
    return dict(w_main=w_main, w_grep=w_grep, wout=w_out.astype(BF16))


def _c_front(x, pc):
    outs = [(1024, F32)] + [(128, F32)] * 6 + [(128, BF16)] * 4 + [(1024, F32)]
    return _tok_call(_c_front_body, x.shape[0], [x], [pc["w_main"]], outs)


def _c_back_body(x_ref, oc_ref, os_ref, ow_ref, og_ref, wg_ref, wo_ref, lg_ref, lb_ref, o_ref):
    x = x_ref[...]
    gates = _sigmoid(_dot(x.astype(BF16), wg_ref[...]))
    o = gates[:, :1024] * oc_ref[...] + gates[:, 1024:2048] * os_ref[...] + gates[:, 2048:] * ow_ref[...]
    out = _dot((o * _silu(og_ref[...])).astype(BF16), wo_ref[...])
    o_ref[...] = _deepnorm_ln(x, out, lg_ref[...], lb_ref[...])


def _c_back(x, o_c, o_s, o_w, og, pc, ln_g, ln_b):
    (out,) = _tok_call(_c_back_body, x.shape[0], [x, o_c, o_s, o_w, og],
                       [pc["w_grep"], pc["wout"], ln_g.reshape(1, -1), ln_b.reshape(1, -1)], [(D_MODEL, F32)])
    return out


def _online_update(s, mask, v_fn, m_ref, l_ref, acc_ref):
    if mask is not None:
        s = jnp.where(mask, s, NEG)
    m_prev = m_ref[...]
    m_new = jnp.maximum(m_prev, jnp.max(s, axis=1, keepdims=True))
    p = jnp.exp(s - m_new)
    if mask is not None:
        p = jnp.where(mask, p, 0.0)
    corr = jnp.exp(m_prev - m_new)
    l_ref[...] = l_ref[...] * corr + jnp.sum(p, axis=1, keepdims=True)
    acc_ref[...] = acc_ref[...] * corr + v_fn(p.astype(BF16))
    m_ref[...] = m_new


NEW_PAD = 16


def _pad_rows16(x):
    return jnp.concatenate([x, jnp.zeros((NEW_PAD - x.shape[0], x.shape[1]), x.dtype)], axis=0)


def _flash_init(m_ref, l_ref, acc_ref):
    m_ref[...] = jnp.full(m_ref.shape, NEG, F32)
    l_ref[...] = jnp.zeros(l_ref.shape, F32)
    acc_ref[...] = jnp.zeros(acc_ref.shape, F32)


def _flash_result(l_ref, acc_ref):
    l = l_ref[...]
    return acc_ref[...] / jnp.where(l > 0.0, l, 1.0)


def _causal_schedule(n_q, tq, tk):
    qi, ki = [], []
    for i in range(n_q):
        last = (i * tq + tq - 1) // tk
        for k in range(last + 1):
            qi.append(i)
            ki.append(k)
    return jnp.asarray(np.array(qi, np.int32)), jnp.asarray(np.array(ki, np.int32))


def _key_tile(seq):
    return 512 if seq % 512 == 0 else TQ


def _mla_prompt_body(qi_ref, ki_ref, ql_ref, qr_ref, lat_ref, kr_ref, o_ref, m_ref, l_ref, acc_ref, *, tk):
    step = pl.program_id(0)
    qi, ki = qi_ref[step], ki_ref[step]

    @pl.when(ki == 0)
    def _():
        _flash_init(m_ref, l_ref, acc_ref)

    lat = lat_ref[...]
    s = _dot_nt(ql_ref[...], lat) + _dot_nt(qr_ref[...], kr_ref[...])
    rows = s.shape[0]
    tok = qi * TQ + (lax.broadcasted_iota(jnp.int32, (rows, tk), 0) >> 3)
    key = ki * tk + lax.broadcasted_iota(jnp.int32, (rows, tk), 1)
    _online_update(s, key <= tok, lambda p: _dot(p, lat), m_ref, l_ref, acc_ref)

    @pl.when(ki == (qi * TQ + TQ - 1) // tk)
    def _():
        o_ref[...] = _flash_result(l_ref, acc_ref).astype(o_ref.dtype)


def _mla_prompt(ql, qr, latb, krb):
    seq = latb.shape[0]
    tk = _key_tile(seq)
    rows = TQ * MLA_HEADS
    qi, ki = _causal_schedule(seq // TQ, TQ, tk)
    grid_spec = pltpu.PrefetchScalarGridSpec(
        num_scalar_prefetch=2, grid=(qi.shape[0],),
        in_specs=[pl.BlockSpec((rows, KV_RANK), lambda s, q, k: (q[s], 0)),
                  pl.BlockSpec((rows, ROPE), lambda s, q, k: (q[s], 0)),
                  pl.BlockSpec((tk, KV_RANK), lambda s, q, k: (k[s], 0)),
                  pl.BlockSpec((tk, ROPE), lambda s, q, k: (k[s], 0))],
        out_specs=pl.BlockSpec((rows, KV_RANK), lambda s, q, k: (q[s], 0)),
        scratch_shapes=[pltpu.VMEM((rows, 1), F32), pltpu.VMEM((rows, 1), F32), pltpu.VMEM((rows, KV_RANK), F32)])
    return pl.pallas_call(
        functools.partial(_mla_prompt_body, tk=tk), grid_spec=grid_spec,
        out_shape=jax.ShapeDtypeStruct((seq * MLA_HEADS, KV_RANK), BF16),
        compiler_params=_cparams(("arbitrary",)))(qi, ki, ql, qr, latb, krb)


def _pages_per_step(n_pages):
    for pp in (16, 8, 4, 2):
        if n_pages % pp == 0:
            return pp
    return 1


def _mla_sample_body(pt_ref, ql_ref, qr_ref, latn_ref, krn_ref, *refs, pp, n_new):
    lat_refs, kr_refs = refs[:pp], refs[pp:2 * pp]
    o_ref, m_ref, l_ref, acc_ref = refs[2 * pp:]
    pc = pl.program_id(1)

    @pl.when(pc == 0)
    def _():
        _flash_init(m_ref, l_ref, acc_ref)

    ql, qr = ql_ref[...], qr_ref[...]
    lats = [r[...].astype(BF16) for r in lat_refs]
    s = jnp.concatenate([_dot_nt(ql, lats[j]) + _dot(qr, kr_refs[j][...].astype(BF16)) for j in range(pp)], axis=1)

    def pv(p):
        acc = _dot(p[:, :PAGE], lats[0])
        for j in range(1, pp):
            acc = acc + _dot(p[:, j * PAGE:(j + 1) * PAGE], lats[j])
        return acc

    _online_update(s, None, pv, m_ref, l_ref, acc_ref)

    @pl.when(pc == pl.num_programs(1) - 1)
    def _():
        latn = _pad_rows16(latn_ref[...]).astype(BF16)
        sn = _dot_nt(ql, latn) + _dot_nt(qr, _pad_rows16(krn_ref[...]).astype(BF16))
        rows = sn.shape[0]
        tok = lax.broadcasted_iota(jnp.int32, (rows, NEW_PAD), 0) >> 3
        key = lax.broadcasted_iota(jnp.int32, (rows, NEW_PAD), 1)
        _online_update(sn, key <= tok, lambda p: _dot(p, latn), m_ref, l_ref, acc_ref)
        o_ref[...] = _flash_result(l_ref, acc_ref).astype(o_ref.dtype)


def _mla_sample(ql, qr, lat_new, kr_new, pool_lat, pool_kr, li, page_table, n_new):
    db, n_pages = page_table.shape
    pp = _pages_per_step(n_pages)
    rows = n_new * MLA_HEADS
    row_spec = lambda w: pl.BlockSpec((rows, w), lambda b, c, pt: (b, 0))
    new_spec = lambda w: pl.BlockSpec((n_new, w), lambda b, c, pt: (b, 0))

    def page_spec(shape, j):
        return pl.BlockSpec((None, None) + shape, lambda b, c, pt, _j=j: (li, pt[b, c * pp + _j], 0, 0))

    pool_kr = jnp.transpose(pool_kr, (0, 1, 3, 2))
    grid_spec = pltpu.PrefetchScalarGridSpec(
        num_scalar_prefetch=1, grid=(db, n_pages // pp),
        in_specs=[row_spec(KV_RANK), row_spec(ROPE), new_spec(KV_RANK), new_spec(ROPE)]
        + [page_spec((PAGE, KV_RANK), j) for j in range(pp)] + [page_spec((ROPE, PAGE), j) for j in range(pp)],
        out_specs=row_spec(KV_RANK),
        scratch_shapes=[pltpu.VMEM((rows, 1), F32), pltpu.VMEM((rows, 1), F32), pltpu.VMEM((rows, KV_RANK), F32)])
    return pl.pallas_call(
        functools.partial(_mla_sample_body, pp=pp, n_new=n_new), grid_spec=grid_spec,
        out_shape=jax.ShapeDtypeStruct((db * rows, KV_RANK), BF16),
        compiler_params=_cparams(("parallel", "arbitrary")))(
            page_table, ql, qr, lat_new, kr_new, *([pool_lat] * pp), *([pool_kr] * pp))


def _ssd_body(xbc_ref, dt_ref, dtt_ref, conv0_ref, h0_ref, cw_ref, cb_ref, alr_ref, alc_ref, dsk_ref,
              y_ref, convn_ref, hn_ref, xpad_ref, h_ref, *, q):
    c = pl.program_id(1)
    keep = CONV_W - 1

    @pl.when(c == 0)
    def _():
        h_ref[...] = h0_ref[...]
        xpad_ref[8 - keep:8, :] = conv0_ref[...]

    xpad_ref[8:8 + q, :] = xbc_ref[...]
    conv = cb_ref[...] + xpad_ref[8 - keep:8 - keep + q, :] * cw_ref[0:1, :]
    for w in range(1, CONV_W):
        conv = conv + xpad_ref[8 - keep + w:8 - keep + w + q, :] * cw_ref[w:w + 1, :]
    tail = xpad_ref[8 + q - keep:8 + q, :]
    convn_ref[...] = tail
    xpad_ref[8 - keep:8, :] = tail

    u = _silu(conv)
    xs = u[:, :D_INNER]
    gn = SSM_GROUPS * D_STATE
    bm, cm = u[:, D_INNER:D_INNER + gn], u[:, D_INNER + gn:]

    dt = dt_ref[...]
    dtt = dtt_ref[...]
    a_row = -jnp.exp(alr_ref[...])
    a_col = -jnp.exp(alc_ref[...])
    ii = lax.broadcasted_iota(jnp.int32, (q, q), 0)
    jj = lax.broadcasted_iota(jnp.int32, (q, q), 1)
    tril = ii >= jj
    cs = jnp.dot(tril.astype(F32), dt * a_row, precision=lax.Precision.HIGHEST, preferred_element_type=F32)
    cst = jnp.dot(dtt * a_col, (ii <= jj).astype(F32), precision=lax.Precision.HIGHEST, preferred_element_type=F32)
    rpg = SSM_HEADS // SSM_GROUPS
    for g in range(SSM_GROUPS):
        bg = bm[:, g * D_STATE:(g + 1) * D_STATE]
        cg = cm[:, g * D_STATE:(g + 1) * D_STATE].astype(BF16)
        cb = _dot_nt(cg, bg.astype(BF16))
        for r in range(rpg):
            h = g * rpg + r
            cs_col, cs_row = cs[:, h:h + 1], cst[h:h + 1, :]
            dt_col, dt_row = dt[:, h:h + 1], dtt[h:h + 1, :]
            cs_last = cst[h:h + 1, q - 1:q]
            decay = jnp.where(tril, jnp.exp(jnp.minimum(cs_col - cs_row, 0.0)), 0.0)
            lmat = (cb * decay * dt_row).astype(BF16)
            xs_h = xs[:, h * SSM_HEADDIM:(h + 1) * SSM_HEADDIM]
            xs_hb = xs_h.astype(BF16)
            y_diag = _dot(lmat, xs_hb)
            h_prev = h_ref[h]
            y_off = _dot_nt(cg, h_prev.astype(BF16)) * jnp.exp(cs_col)
            y_ref[:, h * SSM_HEADDIM:(h + 1) * SSM_HEADDIM] = y_diag + y_off + xs_h * dsk_ref[0:1, h:h + 1]
            bw = (bg * (jnp.exp(cs_last - cs_col) * dt_col)).astype(BF16)
            h_ref[h] = h_prev * jnp.exp(cs_last) + _dot_tn(xs_hb, bw)
    hn_ref[...] = h_ref[...]


def _ssd(xbc, dt, conv0, h0, conv_w, conv_b, a_log, d_skip, batch):
    rows = xbc.shape[0]
    length = rows // batch
    q = SSD_CHUNK if length % SSD_CHUNK == 0 else length
    nc = length // q
    dtt = jnp.transpose(dt.reshape(batch * nc, q, SSM_HEADS), (0, 2, 1))
    keep = CONV_W - 1
    in_specs = [pl.BlockSpec((q, CONV_DIM), lambda b, c: (b * nc + c, 0)),
                pl.BlockSpec((q, SSM_HEADS), lambda b, c: (b * nc + c, 0)),
                pl.BlockSpec((None, SSM_HEADS, q), lambda b, c: (b * nc + c, 0, 0)),
                pl.BlockSpec((None, keep, CONV_DIM), lambda b, c: (b, 0, 0)),
                pl.BlockSpec((None, SSM_HEADS, SSM_HEADDIM, D_STATE), lambda b, c: (b, 0, 0, 0)),
                _const_spec((CONV_W, CONV_DIM)), _const_spec((1, CONV_DIM)), _const_spec((1, SSM_HEADS)),
                _const_spec((SSM_HEADS, 1)), _const_spec((1, SSM_HEADS))]
    out_specs = [pl.BlockSpec((q, D_INNER), lambda b, c: (b * nc + c, 0)),
                 pl.BlockSpec((None, keep, CONV_DIM), lambda b, c: (b, 0, 0)),
                 pl.BlockSpec((None, SSM_HEADS, SSM_HEADDIM, D_STATE), lambda b, c: (b, 0, 0, 0))]
    out_shape = [jax.ShapeDtypeStruct((rows, D_INNER), F32), jax.ShapeDtypeStruct((batch, keep, CONV_DIM), F32),
                 jax.ShapeDtypeStruct((batch, SSM_HEADS, SSM_HEADDIM, D_STATE), F32)]
    return pl.pallas_call(
        functools.partial(_ssd_body, q=q), grid=(batch, nc), in_specs=in_specs, out_specs=out_specs,
        out_shape=out_shape,
        scratch_shapes=[pltpu.VMEM((8 + q, CONV_DIM), F32), pltpu.VMEM((SSM_HEADS, SSM_HEADDIM, D_STATE), F32)],
        compiler_params=_cparams(("parallel", "arbitrary")))(
            xbc, dt, dtt, conv0, h0, conv_w, conv_b.reshape(1, -1), a_log.reshape(1, -1), a_log.reshape(-1, 1),
            d_skip.reshape(1, -1))


CHUNK_W = CMP_STRIDE * NSA_GROUPS * NSA_D
KC_PAD = 16
NEAR = 32


def _prep_cmp(pe, w1, w2):
    eye = jnp.eye(NSA_GROUPS, dtype=w1.dtype)

    def big(w):
        return (w[:, None, :, None, :] * eye[None, :, None, :, None]).reshape(CHUNK_W, NSA_GROUPS * NSA_D)

    w_big = jnp.concatenate([big(w1[:CMP_STRIDE]), big(w1[CMP_STRIDE:])], axis=1).astype(BF16)
    peb = jnp.einsum('ld,ldh->h', pe, w1)
    peb2 = jnp.tile(peb, NSA_GROUPS).reshape(1, -1)
    w2_bd = (w2[None, :, None, :] * eye[:, None, :, None]).reshape(NSA_GROUPS * NSA_D, NSA_GROUPS * NSA_D).astype(BF16)
    return w_big, peb2, w2_bd


def _mm_body(x_ref, w_ref, o_ref):
    o_ref[...] = _dot(x_ref[...].astype(BF16), w_ref[...])


def _cmp_u_prompt(rows_kv, w_big):
    chunks = rows_kv.reshape(-1, CHUNK_W)
    (u,) = _tok_call(_mm_body, chunks.shape[0], [chunks], [w_big], [(w_big.shape[1], F32)])
    return u


def _feature_major(pool):
    return jnp.transpose(pool, (0, 1, 3, 4, 2))


def _cmp_u_paged_body(pt_ref, w_ref, *refs, pp):
    o_ref, tok_ref = refs[pp], refs[pp + 1]
    feat = NSA_GROUPS * NSA_D
    cpp = PAGE // CMP_STRIDE
    for j in range(pp):
        tok_ref[j * PAGE:(j + 1) * PAGE, :] = refs[j][...].reshape(feat, PAGE).T
    acc = None
    for l in range(CMP_STRIDE):
        x = tok_ref[pl.ds(l, pp * cpp, stride=CMP_STRIDE), :].astype(BF16)
        part = _dot(x, w_ref[l * feat:(l + 1) * feat, :])
        acc = part if acc is None else acc + part
    o_ref[...] = acc


def _cmp_u_paged(pool, li, page_table, w_big):
    db, n_pages = page_table.shape
    pp = _pages_per_step(n_pages)
    cpp = PAGE // CMP_STRIDE
    pool_t = _feature_major(pool)
    width = w_big.shape[1]
    grid_spec = pltpu.PrefetchScalarGridSpec(
        num_scalar_prefetch=1, grid=(db, n_pages // pp),
        in_specs=[pl.BlockSpec(w_big.shape, lambda b, c, pt: (0, 0))]
        + [pl.BlockSpec((None, None, NSA_GROUPS, NSA_D, PAGE),
                        lambda b, c, pt, _j=j: (li, pt[b, c * pp + _j], 0, 0, 0)) for j in range(pp)],
        out_specs=pl.BlockSpec((None, pp * cpp, width), lambda b, c, pt: (b, c, 0)),
        scratch_shapes=[pltpu.VMEM((pp * PAGE, NSA_GROUPS * NSA_D), F32)])
    return pl.pallas_call(
        functools.partial(_cmp_u_paged_body, pp=pp), grid_spec=grid_spec,
        out_shape=jax.ShapeDtypeStruct((db, n_pages * cpp, width), F32),
        compiler_params=_cparams(("parallel", "arbitrary")))(page_table, w_big, *([pool_t] * pp))


def _cmp_finish_body(uk_ref, uv_ref, pk_ref, pv_ref, wk_ref, wv_ref, kc_ref, vc_ref, *, n):
    half = NSA_GROUPS * NSA_D

    def fin(u_ref, p_ref, w_ref, o_ref):
        u = u_ref[...]
        hid = u[:, :half] + pltpu.roll(u[:, half:], n - 1, 0) + p_ref[...]
        o_ref[...] = jnp.zeros(o_ref.shape, F32)
        o_ref[KC_PAD:KC_PAD + n, :] = _dot(_silu(hid).astype(BF16), w_ref[...])

    fin(uk_ref, pk_ref, wk_ref, kc_ref)
    fin(uv_ref, pv_ref, wv_ref, vc_ref)


def _cmp_finish(uk, uv, pk, pv, batch):
    n = uk.shape[0] // batch
    half = NSA_GROUPS * NSA_D
    n_out = KC_PAD + n + NEAR
    u_spec = pl.BlockSpec((n, 2 * half), lambda b: (b, 0))
    o_spec = pl.BlockSpec((None, n_out, half), lambda b: (b, 0, 0))
    consts = [pk[1], pv[1], pk[2], pv[2]]
    return pl.pallas_call(
        functools.partial(_cmp_finish_body, n=n), grid=(batch,),
        in_specs=[u_spec, u_spec] + [_const_spec(a.shape) for a in consts], out_specs=[o_spec, o_spec],
        out_shape=[jax.ShapeDtypeStruct((batch, n_out, half), F32)] * 2,
        compiler_params=_cparams(("parallel",)))(uk, uv, *consts)


def _stack_heads(q, g):
    t = q.shape[0]
    lane = lax.broadcasted_iota(jnp.int32, (t, 128), 1)
    lo = lane < NSA_D
    parts = []
    for m in range(g * 4, g * 4 + 4):
        x = q[:, 128 * m:128 * (m + 1)]
        xr = pltpu.roll(x, NSA_D, 1)
        if g == 0:
            parts += [jnp.where(lo, x, 0.0), jnp.where(lo, xr, 0.0)]
        else:
            parts += [jnp.where(lo, 0.0, xr), jnp.where(lo, 0.0, x)]
    return jnp.concatenate(parts, axis=0)


def _unstack_heads(o, g, t):
    lane = lax.broadcasted_iota(jnp.int32, (t, 128), 1)
    lo = lane < NSA_D
    blocks = []
    for m in range(4):
        a, b = o[(2 * m) * t:(2 * m + 1) * t], o[(2 * m + 1) * t:(2 * m + 2) * t]
        if g == 0:
            blocks.append(jnp.where(lo, a, pltpu.roll(b, NSA_D, 1)))
        else:
            blocks.append(jnp.where(lo, pltpu.roll(a, NSA_D, 1), b))
    return blocks


def _bias_rows(rel_bias, dist):
    t, k = dist.shape
    b = rel_bias[jnp.asarray(_t5_bucket_np(dist))]
    return jnp.transpose(b, (2, 0, 1)).reshape(NSA_GROUPS, NSA_HPG * t, k).astype(F32)


def _far_bias(rel_bias, t):
    col = jnp.repeat(rel_bias[NUM_BUCKETS - 1], t)
    return col.reshape(NSA_GROUPS, NSA_HPG * t, 1).astype(F32)


def _topk_mask(score, lane_f, n_rounds):
    sel = jnp.zeros(score.shape, F32)
    for _ in range(n_rounds):
        mx = jnp.max(score, axis=1, keepdims=True)
        idx = jnp.min(jnp.where(score == mx, lane_f, 1e9), axis=1, keepdims=True)
        pick = lane_f == idx
        sel = jnp.where(pick, 1.0, sel)
        score = jnp.where(pick, -2.0, score)
    return sel


def _select_scores(imp, tpos, n_slc):
    lane = lax.broadcasted_iota(jnp.int32, imp.shape, 1)
    cur = tpos >> 6
    forced = (lane == 0) | (lane == cur) | (lane == cur - 1)
    score = jnp.where(lane <= cur, jnp.where(forced, 1e30, imp), -1.0)
    score = jnp.where(lane < n_slc, score, -10.0)
    return score, lane.astype(F32)


def _cmp_prompt_body(q_ref, kc_ref, vc_ref, ov_ref, bn_ref, bf_ref, oc_ref, sel_ref, *, n_cmp, n_slc):
    i = pl.program_id(0)
    q = q_ref[...]
    kc_all, vc_all = kc_ref[...].astype(BF16), vc_ref[...].astype(BF16)
    n_rows = kc_all.shape[0]
    start = pl.multiple_of(i * (TQ // CMP_STRIDE), 8)
    kc_near = kc_ref[pl.ds(start, NEAR), :].astype(BF16)
    vc_near = vc_ref[pl.ds(start, NEAR), :].astype(BF16)
    ov_all = ov_ref[...].astype(BF16)
    ov_near = ov_ref[pl.ds(start, NEAR), :].astype(BF16)
    rows = NSA_HPG * TQ
    colf = lax.broadcasted_iota(jnp.int32, (rows, n_rows), 1)
    mask_f = (colf >= KC_PAD) & (colf < start)
    t_n = lax.broadcasted_iota(jnp.int32, (rows, NEAR), 0) & (TQ - 1)
    jj = lax.broadcasted_iota(jnp.int32, (rows, NEAR), 1)
    jabs = start - KC_PAD + jj
    mask_n = (t_n - CMP_STRIDE * (jj - KC_PAD) - (CMP_LEN - 1) >= 0) & (jabs >= 0) & (jabs < n_cmp)
    tpos = i * TQ + lax.broadcasted_iota(jnp.int32, (TQ, 1), 0)
    for g in range(NSA_GROUPS):
        qg = _stack_heads(q, g).astype(BF16)
        s_f = jnp.where(mask_f, _dot_nt(qg, kc_all) + bf_ref[g], NEG)
        s_n = jnp.where(mask_n, _dot_nt(qg, kc_near) + bn_ref[g], NEG)
        m = jnp.maximum(jnp.max(s_f, axis=1, keepdims=True), jnp.max(s_n, axis=1, keepdims=True))
        p_f = jnp.where(mask_f, jnp.exp(s_f - m), 0.0)
        p_n = jnp.where(mask_n, jnp.exp(s_n - m), 0.0)
        l = jnp.sum(p_f, axis=1, keepdims=True) + jnp.sum(p_n, axis=1, keepdims=True)
        inv = 1.0 / jnp.where(l > 0.0, l, 1.0)
        p_f, p_n = p_f * inv, p_n * inv
        o = _dot(p_f.astype(BF16), vc_all) + _dot(p_n.astype(BF16), vc_near)
        for m4, blk in enumerate(_unstack_heads(o, g, TQ)):
            col = (g * 4 + m4) * 128
            oc_ref[:, col:col + 128] = blk
        ps_f, ps_n = p_f[:TQ], p_n[:TQ]
        for r in range(1, NSA_HPG):
            ps_f = ps_f + p_f[r * TQ:(r + 1) * TQ]
            ps_n = ps_n + p_n[r * TQ:(r + 1) * TQ]
        imp = _dot(ps_f.astype(BF16), ov_all) + _dot(ps_n.astype(BF16), ov_near)
        score, lane_f = _select_scores(imp, tpos, n_slc)
        sel_ref[g] = _topk_mask(score, lane_f, min(N_SEL, n_slc)).astype(sel_ref.dtype)


def _overlap_np(n_rows, n_cmp, n_slc, width):
    ov = np.zeros((n_rows, width), np.float32)
    cs = np.arange(n_cmp)[:, None] * CMP_STRIDE
    bs = np.arange(n_slc)[None, :] * SLC_BLOCK
    ov[KC_PAD:KC_PAD + n_cmp, :n_slc] = np.maximum(
        np.minimum(cs + CMP_LEN, bs + SLC_BLOCK) - np.maximum(cs, bs), 0)
    return ov


def _cmp_prompt(q, kc, vc, rel_bias, seq):
    n_rows = kc.shape[0]
    n_cmp = (seq - CMP_LEN) // CMP_STRIDE + 1
    n_slc = -(-seq // SLC_BLOCK)
    w_slc = -(-n_slc // 128) * 128
    ov = jnp.asarray(_overlap_np(n_rows, n_cmp, n_slc, w_slc))
    t = np.arange(TQ)[:, None]
    jj = np.arange(NEAR)[None, :]
    b_near = _bias_rows(rel_bias, t - CMP_STRIDE * (jj - KC_PAD) - (CMP_LEN - 1))
    b_far = _far_bias(rel_bias, TQ)
    rows = NSA_HPG * TQ
    return pl.pallas_call(
        functools.partial(_cmp_prompt_body, n_cmp=n_cmp, n_slc=n_slc), grid=(seq // TQ,),
        in_specs=[pl.BlockSpec((TQ, 1024), lambda i: (i, 0)), _const_spec(kc.shape), _const_spec(vc.shape),
                  _const_spec(ov.shape), _const_spec(b_near.shape), _const_spec(b_far.shape)],
        out_specs=[pl.BlockSpec((TQ, 1024), lambda i: (i, 0)), pl.BlockSpec((NSA_GROUPS, TQ, w_slc), lambda i: (0, i, 0))],
        out_shape=[jax.ShapeDtypeStruct((seq, 1024), F32), jax.ShapeDtypeStruct((NSA_GROUPS, seq, w_slc), BF16)],
        compiler_params=_cparams(("parallel",)))(q, kc, vc, ov, b_near, b_far)


def _cmp_sample_body(q_ref, kc_ref, vc_ref, ov_ref, bias_ref, oc_ref, sel_ref, *, n_cmp, n_slc, past, t):
    q = q_ref[...]
    kc_all, vc_all = kc_ref[...].astype(BF16), vc_ref[...].astype(BF16)
    ov_all = ov_ref[...].astype(BF16)
    n_rows = kc_all.shape[0]
    rows = NSA_HPG * t
    col = lax.broadcasted_iota(jnp.int32, (rows, n_rows), 1)
    tok = lax.broadcasted_iota(jnp.int32, (rows, n_rows), 0) & (t - 1)
    j = col - KC_PAD
    mask = (j >= 0) & (j < n_cmp) & (past + tok - CMP_STRIDE * j - (CMP_LEN - 1) >= 0)
    tpos = past + lax.broadcasted_iota(jnp.int32, (t, 1), 0)
    for g in range(NSA_GROUPS):
        qg = _stack_heads(q, g).astype(BF16)
        s = jnp.where(mask, _dot_nt(qg, kc_all) + bias_ref[g], NEG)
        m = jnp.max(s, axis=1, keepdims=True)
        p = jnp.where(mask, jnp.exp(s - m), 0.0)
        l = jnp.sum(p, axis=1, keepdims=True)
        p = p * (1.0 / jnp.where(l > 0.0, l, 1.0))
        o = _dot(p.astype(BF16), vc_all)
        for m4, blk in enumerate(_unstack_heads(o, g, t)):
            c0 = (g * 4 + m4) * 128
            oc_ref[:, c0:c0 + 128] = blk
        ps = p[:t]
        for r in range(1, NSA_HPG):
            ps = ps + p[r * t:(r + 1) * t]
        imp = _dot(ps.astype(BF16), ov_all)
        score, lane_f = _select_scores(imp, tpos, n_slc)
        sel_ref[g] = _topk_mask(score, lane_f, min(N_SEL, n_slc)).astype(sel_ref.dtype)


def _cmp_sample(q, kc, vc, rel_bias, past, t):
    db, n_rows = kc.shape[0], kc.shape[1]
    total = past + t
    n_cmp = (total - CMP_LEN) // CMP_STRIDE + 1
    n_slc = -(-total // SLC_BLOCK)
    w_slc = -(-n_slc // 128) * 128
    ov = jnp.asarray(_overlap_np(n_rows, n_cmp, n_slc, w_slc))
    tt = np.arange(t)[:, None]
    jn = np.arange(n_rows)[None, :] - KC_PAD
    bias = _bias_rows(rel_bias, past + tt - CMP_STRIDE * jn - (CMP_LEN - 1))
    return pl.pallas_call(
        functools.partial(_cmp_sample_body, n_cmp=n_cmp, n_slc=n_slc, past=past, t=t), grid=(db,),
        in_specs=[pl.BlockSpec((t, 1024), lambda b: (b, 0)), pl.BlockSpec((None, n_rows, 128), lambda b: (b, 0, 0)),
                  pl.BlockSpec((None, n_rows, 128), lambda b: (b, 0, 0)), _const_spec(ov.shape), _const_spec(bias.shape)],
        out_specs=[pl.BlockSpec((t, 1024), lambda b: (b, 0)),
                   pl.BlockSpec((None, NSA_GROUPS, t, w_slc), lambda b: (b, 0, 0, 0))],
        out_shape=[jax.ShapeDtypeStruct((db * t, 1024), F32), jax.ShapeDtypeStruct((db, NSA_GROUPS, t, w_slc), F32)],
        compiler_params=_cparams(("parallel",)))(q, kc, vc, ov, bias)


def _slc_prompt_body(qi_ref, ki_ref, q_ref, ks_ref, vs_ref, sel_ref, bn_ref, bf_ref, o_ref,
                     qs_ref, m_ref, l_ref, acc_ref, *, tk):
    step = pl.program_id(0)
    qi, ki = qi_ref[step], ki_ref[step]
    rows = NSA_HPG * TQ
    sub = tk // TQ
    bpt = tk // SLC_BLOCK

    @pl.when(ki == 0)
    def _():
        q = q_ref[...]
        for g in range(NSA_GROUPS):
            qs_ref[g] = _stack_heads(q, g).astype(BF16)
            _flash_init(m_ref.at[g], l_ref.at[g], acc_ref.at[g])

    ks, vs = ks_ref[...], vs_ref[...]
    w_slc = sel_ref.shape[2]
    nn = lax.broadcasted_iota(jnp.int32, (w_slc, tk), 0)
    cc = lax.broadcasted_iota(jnp.int32, (w_slc, tk), 1)
    expand = jnp.where(nn == ki * bpt + (cc >> 6), 1.0, 0.0).astype(BF16)
    near = (ki * sub + sub - 1) >= qi - 1

    def run(is_near):
        for g in range(NSA_GROUPS):
            selx = _dot(sel_ref[g], expand)
            s = _dot_nt(qs_ref[g], ks)
            if is_near:
                tok = qi * TQ + lax.broadcasted_iota(jnp.int32, (TQ, tk), 0)
                key = ki * tk + lax.broadcasted_iota(jnp.int32, (TQ, tk), 1)
                selx = jnp.where(key <= tok, selx, 0.0)
                pieces = []
                for u in range(sub):
                    delta = qi - (ki * sub + u)
                    far = jnp.broadcast_to(bf_ref[g], (rows, TQ))
                    pieces.append(jnp.where(delta == 0, bn_ref[g, 0], jnp.where(delta == 1, bn_ref[g, 1], far)))
                s = s + jnp.concatenate(pieces, axis=1)
            else:
                s = s + bf_ref[g]
            mask = jnp.concatenate([selx] * NSA_HPG, axis=0) > 0.5
            _online_update(s, mask, lambda p: _dot(p, vs), m_ref.at[g], l_ref.at[g], acc_ref.at[g])

    @pl.when(near)
    def _():
        run(True)

    @pl.when(jnp.logical_not(near))
    def _():
        run(False)

    @pl.when(ki == (qi * TQ + TQ - 1) // tk)
    def _():
        for g in range(NSA_GROUPS):
            o = _flash_result(l_ref.at[g], acc_ref.at[g])
            for m4, blk in enumerate(_unstack_heads(o, g, TQ)):
                col = (g * 4 + m4) * 128
                o_ref[:, col:col + 128] = blk


def _tile_bias(rel_bias):
    t = np.arange(TQ)[:, None]
    c = np.arange(TQ)[None, :]
    return jnp.stack([_bias_rows(rel_bias, d * TQ + t - c) for d in (0, 1)], axis=1)


def _slc_prompt(q, ksb, vsb, sel, rel_bias, seq):
    tk = _key_tile(seq)
    rows = NSA_HPG * TQ
    qi, ki = _causal_schedule(seq // TQ, TQ, tk)
    b_near = _tile_bias(rel_bias)
    b_far = _far_bias(rel_bias, TQ)
    w_slc = sel.shape[2]
    grid_spec = pltpu.PrefetchScalarGridSpec(
        num_scalar_prefetch=2, grid=(qi.shape[0],),
        in_specs=[pl.BlockSpec((TQ, 1024), lambda s, a, b: (a[s], 0)),
                  pl.BlockSpec((tk, 128), lambda s, a, b: (b[s], 0)),
                  pl.BlockSpec((tk, 128), lambda s, a, b: (b[s], 0)),
                  pl.BlockSpec((NSA_GROUPS, TQ, w_slc), lambda s, a, b: (0, a[s], 0)),
                  pl.BlockSpec(b_near.shape, lambda s, a, b: (0, 0, 0, 0)),
                  pl.BlockSpec(b_far.shape, lambda s, a, b: (0, 0, 0))],
        out_specs=pl.BlockSpec((TQ, 1024), lambda s, a, b: (a[s], 0)),
        scratch_shapes=[pltpu.VMEM((NSA_GROUPS, rows, 128), BF16), pltpu.VMEM((NSA_GROUPS, rows, 1), F32),
                        pltpu.VMEM((NSA_GROUPS, rows, 1), F32), pltpu.VMEM((NSA_GROUPS, rows, 128), F32)])
    return pl.pallas_call(
        functools.partial(_slc_prompt_body, tk=tk), grid_spec=grid_spec,
        out_shape=jax.ShapeDtypeStruct((seq, 1024), F32),
        compiler_params=_cparams(("arbitrary",)))(qi, ki, q, ksb, vsb, sel, b_near, b_far)


N_WIN_BLK = WINDOW // TQ + 1


def _win_prompt_body(q_ref, *refs):
    k_refs, v_refs = refs[:N_WIN_BLK], refs[N_WIN_BLK:2 * N_WIN_BLK]
    bn_ref, bf_ref, o_ref = refs[2 * N_WIN_BLK:]
    i = pl.program_id(0)
    q = q_ref[...]
    rows = NSA_HPG * TQ
    n_keys = N_WIN_BLK * TQ
    k = jnp.concatenate([r[...] for r in k_refs], axis=0)
    v = jnp.concatenate([r[...] for r in v_refs], axis=0)
    t = lax.broadcasted_iota(jnp.int32, (rows, n_keys), 0) & (TQ - 1)
    c = lax.broadcasted_iota(jnp.int32, (rows, n_keys), 1)
    dist = WINDOW + t - c
    mask = (dist >= 0) & (dist < WINDOW) & ((i - (N_WIN_BLK - 1)) * TQ + c >= 0)
    for g in range(NSA_GROUPS):
        qg = _stack_heads(q, g).astype(BF16)
        far = jnp.broadcast_to(bf_ref[g], (rows, n_keys - 2 * TQ))
        bias = jnp.concatenate([far, bn_ref[g, 1], bn_ref[g, 0]], axis=1)
        s = jnp.where(mask, _dot_nt(qg, k) + bias, NEG)
        m = jnp.max(s, axis=1, keepdims=True)
        p = jnp.exp(s - m)
        o = _dot(p.astype(BF16), v) / jnp.sum(p, axis=1, keepdims=True)
        for m4, blk in enumerate(_unstack_heads(o, g, TQ)):
            col = (g * 4 + m4) * 128
            o_ref[:, col:col + 128] = blk


def _win_prompt(q, kwb, vwb, rel_bias, seq):
    b_near = _tile_bias(rel_bias)
    b_far = _far_bias(rel_bias, TQ)

    def kv_spec(u):
        return pl.BlockSpec((TQ, 128), lambda i, _u=u: (jnp.maximum(i - (N_WIN_BLK - 1) + _u, 0), 0))

    return pl.pallas_call(
        _win_prompt_body, grid=(seq // TQ,),
        in_specs=[pl.BlockSpec((TQ, 1024), lambda i: (i, 0))] + [kv_spec(u) for u in range(N_WIN_BLK)] * 2
        + [_const_spec(b_near.shape), _const_spec(b_far.shape)],
        out_specs=pl.BlockSpec((TQ, 1024), lambda i: (i, 0)),
        out_shape=jax.ShapeDtypeStruct((seq, 1024), F32),
        compiler_params=_cparams(("parallel",)))(q, *([kwb] * N_WIN_BLK), *([vwb] * N_WIN_BLK), b_near, b_far)


def _slc_sample_body(pt_ref, q_ref, sel_ref, seln_ref, ex_ref, bias_ref, bnew_ref, kn_ref, vn_ref, *refs, pp, t):
    k_refs, v_refs = refs[:pp], refs[pp:2 * pp]
    o_ref, qs_ref, m_ref, l_ref, acc_ref = refs[2 * pp:]
    pc = pl.program_id(1)
    rows = NSA_HPG * t
    tk = pp * PAGE

    @pl.when(pc == 0)
    def _():
        q = q_ref[...]
        for g in range(NSA_GROUPS):
            qs_ref[g] = _stack_heads(q, g).astype(BF16)
            _flash_init(m_ref.at[g], l_ref.at[g], acc_ref.at[g])

    feat = NSA_GROUPS * NSA_D
    ks = [r[...].reshape(feat, PAGE).astype(BF16) for r in k_refs]
    vs = [r[...].reshape(feat, PAGE).astype(BF16) for r in v_refs]
    ex = ex_ref[...]
    for g in range(NSA_GROUPS):
        qg = qs_ref[g]
        s = jnp.concatenate([_dot(qg, ks[j]) for j in range(pp)], axis=1) + bias_ref[g]
        selx = _dot(sel_ref[g].astype(BF16), ex)
        mask = jnp.concatenate([selx] * NSA_HPG, axis=0) > 0.5

        def pv(p):
            acc = _dot_nt(p[:, :PAGE], vs[0])
            for j in range(1, pp):
                acc = acc + _dot_nt(p[:, j * PAGE:(j + 1) * PAGE], vs[j])
            return acc

        _online_update(s, mask, pv, m_ref.at[g], l_ref.at[g], acc_ref.at[g])

    @pl.when(pc == pl.num_programs(1) - 1)
    def _():
        kn, vn = _pad_rows16(kn_ref[...]).astype(BF16), _pad_rows16(vn_ref[...]).astype(BF16)
        tok = lax.broadcasted_iota(jnp.int32, (rows, NEW_PAD), 0) & (t - 1)
        key = lax.broadcasted_iota(jnp.int32, (rows, NEW_PAD), 1)
        for g in range(NSA_GROUPS):
            s = _dot_nt(qs_ref[g], kn) + bnew_ref[g]
            seln = jnp.concatenate([seln_ref[g]] * NSA_HPG, axis=0) > 0.5
            _online_update(s, seln & (key <= tok), lambda p: _dot(p, vn), m_ref.at[g], l_ref.at[g], acc_ref.at[g])
            o = _flash_result(l_ref.at[g], acc_ref.at[g])
            for m4, blk in enumerate(_unstack_heads(o, g, t)):
                col = (g * 4 + m4) * 128
                o_ref[:, col:col + 128] = blk


def _slc_sample(q, sel, ks_new, vs_new, pool_k, pool_v, li, page_table, rel_bias, t):
    db, n_pages = page_table.shape
    past = n_pages * PAGE
    pp = _pages_per_step(n_pages)
    n_steps = n_pages // pp
    tk = pp * PAGE
    bpt = tk // SLC_BLOCK
    rows = NSA_HPG * t
    sel_past = sel[..., :n_steps * bpt].reshape(db, NSA_GROUPS, t, n_steps, bpt)
    sel_past = jnp.transpose(sel_past, (0, 3, 1, 2, 4))
    new_blk = past // SLC_BLOCK
    sel_new = jnp.broadcast_to(sel[..., new_blk:new_blk + 1], (db, NSA_GROUPS, t, NEW_PAD))
    expand = jnp.asarray((np.arange(bpt)[:, None] == np.arange(tk)[None, :] // SLC_BLOCK).astype(np.float32)).astype(BF16)
    tt = np.arange(t)[:, None]
    assert tk >= FAR_DIST
    last_keys = (n_steps - 1) * tk + np.arange(tk)[None, :]
    bias = jnp.stack([jnp.broadcast_to(_far_bias(rel_bias, t), (NSA_GROUPS, rows, tk)),
                      _bias_rows(rel_bias, past + tt - last_keys)])
    b_new = _bias_rows(rel_bias, tt - np.arange(NEW_PAD)[None, :])
    pk, pv = _feature_major(pool_k), _feature_major(pool_v)

    def page_spec(j):
        return pl.BlockSpec((None, None, NSA_GROUPS, NSA_D, PAGE),
                            lambda b, c, pt, _j=j: (li, pt[b, c * pp + _j], 0, 0, 0))

    grid_spec = pltpu.PrefetchScalarGridSpec(
        num_scalar_prefetch=1, grid=(db, n_steps),
        in_specs=[pl.BlockSpec((t, 1024), lambda b, c, pt: (b, 0)),
                  pl.BlockSpec((None, None, NSA_GROUPS, t, bpt), lambda b, c, pt: (b, c, 0, 0, 0)),
                  pl.BlockSpec((None, NSA_GROUPS, t, NEW_PAD), lambda b, c, pt: (b, 0, 0, 0)),
                  pl.BlockSpec(expand.shape, lambda b, c, pt: (0, 0)),
                  pl.BlockSpec((None, NSA_GROUPS, rows, tk),
                               lambda b, c, pt: (jnp.where(c == n_steps - 1, 1, 0), 0, 0, 0)),
                  pl.BlockSpec(b_new.shape, lambda b, c, pt: (0, 0, 0)),
                  pl.BlockSpec((t, 128), lambda b, c, pt: (b, 0)),
                  pl.BlockSpec((t, 128), lambda b, c, pt: (b, 0))]
        + [page_spec(j) for j in range(pp)] * 2,
        out_specs=pl.BlockSpec((t, 1024), lambda b, c, pt: (b, 0)),
        scratch_shapes=[pltpu.VMEM((NSA_GROUPS, rows, 128), BF16), pltpu.VMEM((NSA_GROUPS, rows, 1), F32),
                        pltpu.VMEM((NSA_GROUPS, rows, 1), F32), pltpu.VMEM((NSA_GROUPS, rows, 128), F32)])
    return pl.pallas_call(
        functools.partial(_slc_sample_body, pp=pp, t=t), grid_spec=grid_spec,
        out_shape=jax.ShapeDtypeStruct((db * t, 1024), F32),
        compiler_params=_cparams(("parallel", "arbitrary")))(
            page_table, q, sel_past, sel_new, expand, bias, b_new, ks_new, vs_new, *([pk] * pp), *([pv] * pp))


def _win_sample_body(q_ref, wk_ref, wv_ref, kn_ref, vn_ref, bias_ref, o_ref, *, t, w_buf):
    q = q_ref[...]
    rows = NSA_HPG * t
    feat = NSA_GROUPS * NSA_D
    kt = wk_ref[...].reshape(feat, w_buf).astype(BF16)
    vt = wv_ref[...].reshape(feat, w_buf).astype(BF16)
    kn, vn = _pad_rows16(kn_ref[...]).astype(BF16), _pad_rows16(vn_ref[...]).astype(BF16)
    n_keys = w_buf + NEW_PAD
    tok = lax.broadcasted_iota(jnp.int32, (rows, n_keys), 0) & (t - 1)
    c = lax.broadcasted_iota(jnp.int32, (rows, n_keys), 1)
    dist = w_buf + tok - c
    mask = (dist >= 0) & (dist < WINDOW)
    for g in range(NSA_GROUPS):
        qg = _stack_heads(q, g).astype(BF16)
        s = jnp.concatenate([_dot(qg, kt), _dot_nt(qg, kn)], axis=1) + bias_ref[g]
        s = jnp.where(mask, s, NEG)
        m = jnp.max(s, axis=1, keepdims=True)
        p = jnp.exp(s - m)
        pb = p.astype(BF16)
        o = (_dot_nt(pb[:, :w_buf], vt) + _dot(pb[:, w_buf:], vn)) / jnp.sum(p, axis=1, keepdims=True)
        for m4, blk in enumerate(_unstack_heads(o, g, t)):
            col = (g * 4 + m4) * 128
            o_ref[:, col:col + 128] = blk


def _win_sample(q, wk_buf, wv_buf, kw_new, vw_new, rel_bias, t):
    db, w_buf = wk_buf.shape[0], wk_buf.shape[1]
    tt = np.arange(t)[:, None]
    bias = _bias_rows(rel_bias, w_buf + tt - np.arange(w_buf + NEW_PAD)[None, :])
    wk = jnp.transpose(wk_buf, (0, 2, 3, 1))
    wv = jnp.transpose(wv_buf, (0, 2, 3, 1))
    buf_spec = pl.BlockSpec((None, NSA_GROUPS, NSA_D, w_buf), lambda b: (b, 0, 0, 0))
    return pl.pallas_call(
        functools.partial(_win_sample_body, t=t, w_buf=w_buf), grid=(db,),
        in_specs=[pl.BlockSpec((t, 1024), lambda b: (b, 0)), buf_spec, buf_spec, pl.BlockSpec((t, 128), lambda b: (b, 0)),
                  pl.BlockSpec((t, 128), lambda b: (b, 0)), _const_spec(bias.shape)],
        out_specs=pl.BlockSpec((t, 1024), lambda b: (b, 0)),
        out_shape=jax.ShapeDtypeStruct((db * t, 1024), F32),
        compiler_params=_cparams(("parallel",)))(q, wk, wv, kw_new, vw_new, bias)


def kernel(x_prompt, x_sample, cache_mla_latent, cache_mla_krope, state_ssm, state_conv, cache_cmp_k, cache_cmp_v, cache_slc_k, cache_slc_v, state_win_k, state_win_v, page_table, rel_bias, w_in_a, q_norm, w_uq, kv_norm, w_uk, w_uv, conv_w, conv_b, dt_bias, a_log, d_skip, ssm_norm, w_out_a, ln_a_g, ln_a_b, w_in_c, cmp_pe_k, cmp_w1_k, cmp_w2_k, cmp_pe_v, cmp_w1_v, cmp_w2_v, w_out_c, ln_c_g, ln_c_b):
    bp, sp = x_prompt.shape[:2]
    db, ss = x_sample.shape[:2]
    n_pages = page_table.shape[1]
    past = n_pages * PAGE
    assert bp == 1 and x_prompt.shape[2] == D_MODEL and cache_mla_latent.shape[2] == PAGE
    assert sp % TQ == 0 and sp >= WINDOW and ss == 8
    assert past % SLC_BLOCK == 0 and ss < CMP_STRIDE and past % CMP_STRIDE == 0 and past >= WINDOW
    assert state_win_k.shape[2] == WINDOW

    hp = x_prompt.reshape(sp, D_MODEL)
    hs = x_sample.reshape(db * ss, D_MODEL)
    pos_p = jnp.arange(sp)
    pos_s = jnp.tile(past + jnp.arange(ss), db)
    st = {}

    pa = _prep_a(w_in_a[0], q_norm[0], w_uq[0], kv_norm[0], w_uk[0], w_uv[0], dt_bias[0], w_out_a[0])
    qlat, qrope, lat, latb, kr, krb, g, z, xbc, dt = _a_front(hp, pos_p, pa)
    o_lat = _mla_prompt(qlat.reshape(sp * MLA_HEADS, KV_RANK), qrope.reshape(sp * MLA_HEADS, ROPE), latb, krb)
    y, cbuf, hst = _ssd(xbc, dt, jnp.zeros((1, CONV_W - 1, CONV_DIM), F32),
                        jnp.zeros((1, SSM_HEADS, SSM_HEADDIM, D_STATE), F32), conv_w[0], conv_b[0], a_log[0],
                        d_skip[0], 1)
    hp = _a_back(hp, o_lat.reshape(sp, MLA_HEADS * KV_RANK), g, y, z, pa, ssm_norm[0], ln_a_g[0], ln_a_b[0])
    st['p_lat'], st['p_krope'] = lat.reshape(1, 1, sp, KV_RANK), kr.reshape(1, 1, sp, ROPE)
    st['p_ssm'], st['p_conv'] = hst[None], cbuf[None]
    qlat, qrope, lat, latb, kr, krb, g, z, xbc, dt = _a_front(hs, pos_s, pa)
    o_lat = _mla_sample(qlat.reshape(db * ss * MLA_HEADS, KV_RANK), qrope.reshape(db * ss * MLA_HEADS, ROPE), lat, kr,
                        cache_mla_latent, cache_mla_krope, 0, page_table, ss)
    y, cbuf, hst = _ssd(xbc, dt, state_conv[0], state_ssm[0], conv_w[0], conv_b[0], a_log[0], d_skip[0], db)
    hs = _a_back(hs, o_lat.reshape(db * ss, MLA_HEADS * KV_RANK), g, y, z, pa, ssm_norm[0], ln_a_g[0], ln_a_b[0])
    st['s_lat'], st['s_krope'] = lat.reshape(1, db, ss, KV_RANK), kr.reshape(1, db, ss, ROPE)
    st['s_ssm'], st['s_conv'] = hst[None], cbuf[None]

    pc = _prep_c(w_in_c[0], w_out_c[0])
    pk = _prep_cmp(cmp_pe_k[0], cmp_w1_k[0], cmp_w2_k[0])
    pv = _prep_cmp(cmp_pe_v[0], cmp_w1_v[0], cmp_w2_v[0])
    kv5 = lambda a, b_, s_: a.reshape(1, b_, s_, NSA_GROUPS, NSA_D)
    q, kc_r, vc_r, ks, vs, kw, vw, ksb, vsb, kwb, vwb, og = _c_front(hp, pc)
    kc, vc = _cmp_finish(_cmp_u_prompt(kc_r, pk[0]), _cmp_u_prompt(vc_r, pv[0]), pk, pv, 1)
    o_c, sel = _cmp_prompt(q, kc[0], vc[0], rel_bias, sp)
    o_s = _slc_prompt(q, ksb, vsb, sel, rel_bias, sp)
    o_w = _win_prompt(q, kwb, vwb, rel_bias, sp)
    hp = _c_back(hp, o_c, o_s, o_w, og, pc, ln_c_g[0], ln_c_b[0])
    keep = min(WINDOW, sp)
    for n, a in (('p_cmp_k', kc_r), ('p_cmp_v', vc_r), ('p_slc_k', ks), ('p_slc_v', vs)):
        st[n] = kv5(a, 1, sp)
    st['p_win_k'], st['p_win_v'] = kv5(kw[sp - keep:], 1, keep), kv5(vw[sp - keep:], 1, keep)
    q, kc_r, vc_r, ks, vs, kw, vw, ksb, vsb, kwb, vwb, og = _c_front(hs, pc)
    uk = _cmp_u_paged(cache_cmp_k, 0, page_table, pk[0])
    uv = _cmp_u_paged(cache_cmp_v, 0, page_table, pv[0])
    n_chunks = uk.shape[1]
    kc, vc = _cmp_finish(uk.reshape(db * n_chunks, -1), uv.reshape(db * n_chunks, -1), pk, pv, db)
    o_c, sel = _cmp_sample(q, kc, vc, rel_bias, past, ss)
    o_s = _slc_sample(q, sel, ks, vs, cache_slc_k, cache_slc_v, 0, page_table, rel_bias, ss)
    o_w = _win_sample(q, state_win_k[0], state_win_v[0], kw, vw, rel_bias, ss)
    hs = _c_back(hs, o_c, o_s, o_w, og, pc, ln_c_g[0], ln_c_b[0])
    for n, a in (('s_cmp_k', kc_r), ('s_cmp_v', vc_r), ('s_slc_k', ks), ('s_slc_v', vs)):
        st[n] = kv5(a, db, ss)
    new5 = lambda a: a.reshape(db, ss, NSA_GROUPS, NSA_D)
    st['s_win_k'] = jnp.concatenate([state_win_k[0], new5(kw)], 1)[None, :, ss:]
    st['s_win_v'] = jnp.concatenate([state_win_v[0], new5(vw)], 1)[None, :, ss:]

    names = ['p_lat', 'p_krope', 'p_ssm', 'p_conv', 'p_cmp_k', 'p_cmp_v', 'p_slc_k', 'p_slc_v', 'p_win_k', 'p_win_v',
             's_lat', 's_krope', 's_ssm', 's_conv', 's_cmp_k', 's_cmp_v', 's_slc_k', 's_slc_v', 's_win_k', 's_win_v']
    return (hp.reshape(bp, sp, D_MODEL), hs.reshape(db, ss, D_MODEL)) + tuple(st[n] for n in names)
```

```python
import functools
import math

import numpy as np
import jax
import jax.numpy as jnp
from jax import lax
from jax.experimental import pallas as pl
from jax.experimental.pallas import tpu as pltpu

F32 = jnp.float32
BF16 = jnp.bfloat16

D_MODEL = 1024
DEPTH = 2
DEEPNORM_ALPHA = (2.0 * DEPTH) ** 0.25
LN_EPS = 1e-5
RMS_EPS = 1e-6
NEG = -1e30
PAGE = 128

MLA_HEADS = 8
Q_RANK = 384
KV_RANK = 256
NOPE = 64
ROPE = 32
MLA_V = 64
MLA_WIDTH = MLA_HEADS * MLA_V
MLA_SCALE = (NOPE + ROPE) ** -0.5
ROPE_BASE = 10000.0

SSM_HEADDIM = 64
D_INNER = 1024
SSM_HEADS = 16
SSM_GROUPS = 2
D_STATE = 128
CONV_W = 4
CONV_DIM = D_INNER + 2 * SSM_GROUPS * D_STATE
SSD_CHUNK = 128

NSA_HEADS = 16
NSA_GROUPS = 2
NSA_HPG = 8
NSA_D = 64
NSA_SCALE = NSA_D ** -0.5
CMP_LEN = 32
CMP_STRIDE = 16
SLC_BLOCK = 64
N_SEL = 16
WINDOW = 512
NUM_BUCKETS = 32
MAX_DISTANCE = 128
FAR_DIST = 128

TQ = 128
VMEM_LIMIT = 56 * 1024 * 1024


def _cparams(sem):
    return pltpu.CompilerParams(dimension_semantics=sem, vmem_limit_bytes=VMEM_LIMIT)


def _const_spec(shape):
    nd = len(shape)
    return pl.BlockSpec(shape, lambda *a, _nd=nd: (0,) * _nd)


def _dot(a, b):
    return jnp.dot(a, b, preferred_element_type=F32)


def _dot_nt(a, b):
    return lax.dot_general(a, b, (((1,), (1,)), ((), ())), preferred_element_type=F32)


def _dot_tn(a, b):
    return lax.dot_general(a, b, (((0,), (0,)), ((), ())), preferred_element_type=F32)


def _sigmoid(x):
    return 1.0 / (1.0 + jnp.exp(-x))


def _silu(x):
    return x * _sigmoid(x)


def _rms(x, g):
    return x * lax.rsqrt(jnp.mean(x * x, axis=-1, keepdims=True) + RMS_EPS) * g


def _deepnorm_ln(x, out, g, b):
    h = DEEPNORM_ALPHA * x + out
    mu = jnp.mean(h, axis=-1, keepdims=True)
    d = h - mu
    var = jnp.mean(d * d, axis=-1, keepdims=True)
    return d * lax.rsqrt(var + LN_EPS) * g + b


def _t5_bucket_np(dist):
    n = np.maximum(dist, 0)
    exact = NUM_BUCKETS // 2
    nf = np.maximum(n, exact).astype(np.float32)
    large = exact + (np.log(nf / np.float32(exact)) / np.float32(math.log(MAX_DISTANCE / exact))
                     * np.float32(NUM_BUCKETS - exact)).astype(np.int32)
    return np.where(n < exact, n, np.minimum(large, NUM_BUCKETS - 1)).astype(np.int32)


def _row_tile(rows):
    return 256 if rows % 256 == 0 else rows


def _tok_call(body, rows, row_ins, const_ins, outs):
    tm = _row_tile(rows)
    in_specs = [pl.BlockSpec((tm, a.shape[1]), lambda i: (i, 0)) for a in row_ins]
    in_specs += [_const_spec(a.shape) for a in const_ins]
    out_specs = [pl.BlockSpec((tm, w), lambda i: (i, 0)) for w, _ in outs]
    out_shape = [jax.ShapeDtypeStruct((rows, w), dt) for w, dt in outs]
    return pl.pallas_call(
        body, grid=(rows // tm,), in_specs=in_specs, out_specs=out_specs, out_shape=out_shape,
        name=body.__name__.strip("_").removesuffix("_body"), compiler_params=_cparams(("parallel",)))(*row_ins, *const_ins)


A_CQ, A_CKV, A_G, A_Z, A_XBC, A_MISC, A_END = 0, 384, 640, 1152, 2176, 3712, 3840


def _a_front_body(x_ref, ccq_ref, ssq_ref, cck_ref, ssk_ref, w_ref, qn_ref, kvn_ref, wuq_ref, wuk_ref, dtb_ref,
                  qlat_ref, qrope_ref, lat_ref, latb_ref, kr_ref, krb_ref, g_ref, z_ref, xbc_ref, dt_ref):
    xb = x_ref[...].astype(BF16)

    def proj(lo, hi):
        return _dot(xb, w_ref[:, lo:hi])

    cqn = _rms(proj(A_CQ, A_CKV), qn_ref[...])
    qall = _dot(cqn.astype(BF16), wuq_ref[...])
    qrope = (qall[:, 512:768] * ccq_ref[...] + qall[:, 768:1024] * ssq_ref[...]) * MLA_SCALE
    qrope_ref[...] = qrope.astype(BF16)
    qlat = _dot(qall[:, :512].astype(BF16), wuk_ref[...]) * MLA_SCALE
    qlat_ref[...] = qlat.astype(BF16)
    lat = _rms(proj(A_CKV, A_G), kvn_ref[...])
    lat_ref[...] = lat
    latb_ref[...] = lat.astype(BF16)
    g_ref[...] = proj(A_G, A_Z)
    z_ref[...] = proj(A_Z, A_XBC)
    xbc_ref[...] = proj(A_XBC, A_MISC)
    misc = proj(A_MISC, A_END)
    kr = misc[:, 0:32] * cck_ref[...] + misc[:, 32:64] * ssk_ref[...]
    kr_ref[...] = kr
    krb_ref[...] = kr.astype(BF16)
    v = misc[:, 64:80] + dtb_ref[...]
    dt_ref[...] = jnp.maximum(v, 0.0) + jnp.log1p(jnp.exp(-jnp.abs(v)))


def _rot_cols(w):
    h = ROPE // 2
    return jnp.concatenate([-w[..., h:], w[..., :h]], axis=-1)


def _prep_a(w_in, q_norm, w_uq, kv_norm, w_uk, w_uv, dt_bias, w_out):
    cq, ckv, kr, g, z, xbc, dtw = jnp.split(w_in, np.cumsum([Q_RANK, KV_RANK, ROPE, MLA_WIDTH, D_INNER, CONV_DIM])[:].tolist(),
                                            axis=1)
    pad = jnp.zeros((w_in.shape[0], A_END - A_MISC - 2 * ROPE - SSM_HEADS), w_in.dtype)
    w1 = jnp.concatenate([cq, ckv, g, z, xbc, kr, _rot_cols(kr), dtw, pad], axis=1).astype(BF16)
    wq = w_uq.reshape(Q_RANK, MLA_HEADS, NOPE + ROPE)
    wq_n = wq[:, :, :NOPE].reshape(Q_RANK, MLA_HEADS * NOPE)
    wq_r = wq[:, :, NOPE:]
    wuq = jnp.concatenate([wq_n, wq_r.reshape(Q_RANK, -1), _rot_cols(wq_r).reshape(Q_RANK, -1)], axis=1).astype(BF16)
    eye = jnp.eye(MLA_HEADS, dtype=w_uk.dtype)
    wuk = (jnp.transpose(w_uk, (1, 2, 0))[:, :, None, :] * eye[:, None, :, None]).reshape(
        MLA_HEADS * NOPE, MLA_HEADS * KV_RANK).astype(BF16)
    wuv = (jnp.transpose(w_uv, (1, 0, 2))[:, :, None, :] * eye[:, None, :, None]).reshape(
        MLA_HEADS * KV_RANK, MLA_WIDTH).astype(BF16)
    return dict(w1=w1, qn=q_norm.reshape(1, -1), kvn=kv_norm.reshape(1, -1), wuq=wuq, wuk=wuk, wuv=wuv,
                dtb=dt_bias.reshape(1, -1), wout=w_out.astype(BF16))


def _rope_tables(pos):
    half = ROPE // 2
    inv = ROPE_BASE ** (-jnp.arange(half, dtype=F32) / half)
    ang = pos.astype(F32)[:, None] * inv[None, :]
    c, s = jnp.cos(ang), jnp.sin(ang)
    cck, ssk = jnp.concatenate([c, c], 1), jnp.concatenate([s, s], 1)
    return jnp.tile(cck, (1, MLA_HEADS)), jnp.tile(ssk, (1, MLA_HEADS)), cck, ssk


def _a_front(x, pos, pa):
    rows = x.shape[0]
    ccq, ssq, cck, ssk = _rope_tables(pos)
    outs = [(MLA_HEADS * KV_RANK, BF16), (MLA_HEADS * ROPE, BF16), (KV_RANK, F32), (KV_RANK, BF16), (ROPE, F32),
            (ROPE, BF16), (MLA_WIDTH, F32), (D_INNER, F32), (CONV_DIM, F32), (SSM_HEADS, F32)]
    return _tok_call(_a_front_body, rows, [x, ccq, ssq, cck, ssk],
                     [pa["w1"], pa["qn"], pa["kvn"], pa["wuq"], pa["wuk"], pa["dtb"]], outs)


def _a_back_body(x_ref, ol_ref, g_ref, y_ref, z_ref, wuv_ref, sn_ref, wo_ref, lg_ref, lb_ref, o_ref):
    o_mla = _dot(ol_ref[...], wuv_ref[...])
    a = o_mla * _silu(g_ref[...])
    yn = _rms(y_ref[...] * _silu(z_ref[...]), sn_ref[...])
    out = _dot(a.astype(BF16), wo_ref[:MLA_WIDTH, :]) + _dot(yn.astype(BF16), wo_ref[MLA_WIDTH:, :])
    o_ref[...] = _deepnorm_ln(x_ref[...], out, lg_ref[...], lb_ref[...])


def _a_back(x, o_lat, g, y, z, pa, ssm_norm, ln_g, ln_b):
    (out,) = _tok_call(_a_back_body, x.shape[0], [x, o_lat, g, y, z],
                       [pa["wuv"], ssm_norm.reshape(1, -1), pa["wout"], ln_g.reshape(1, -1), ln_b.reshape(1, -1)],
                       [(D_MODEL, F32)])
    return out


C_Q, C_KV, C_OG, C_END = 0, 1024, 1792, 2816


def _c_front_body(x_ref, w_ref, q_ref, kc_ref, vc_ref, ks_ref, vs_ref, kw_ref, vw_ref, ksb_ref, vsb_ref, kwb_ref,
                  vwb_ref, og_ref):
    xb = x_ref[...].astype(BF16)
    q_ref[...] = _dot(xb, w_ref[:, C_Q:C_KV]) * NSA_SCALE
    kv = _dot(xb, w_ref[:, C_KV:C_OG])
    for j, r in enumerate((kc_ref, vc_ref, ks_ref, vs_ref, kw_ref, vw_ref)):
        r[...] = kv[:, 128 * j:128 * (j + 1)]
    for j, r in enumerate((ksb_ref, vsb_ref, kwb_ref, vwb_ref)):
        r[...] = kv[:, 128 * (j + 2):128 * (j + 3)].astype(BF16)
    og_ref[...] = _dot(xb, w_ref[:, C_OG:C_END])


def _prep_c(w_in, w_out):
    n_kv = 6 * NSA_GROUPS * NSA_D
    w_main = jnp.concatenate([w_in[:, :1024 + n_kv], w_in[:, 1024 + n_kv + 3 * NSA_HEADS:]], axis=1).astype(BF16)
    w_g = w_in[:, 1024 + n_kv:1024 + n_kv + 3 * NSA_HEADS]
    w_grep = jnp.repeat(w_g, NSA_D, axis=1).astype(BF16)
    return dict(w_main=w_main, w_grep=w_grep, wout=w_out.astype(BF16))


def _c_front(x, pc):
    outs = [(1024, F32)] + [(128, F32)] * 6 + [(128, BF16)] * 4 + [(1024, F32)]
    return _tok_call(_c_front_body, x.shape[0], [x], [pc["w_main"]], outs)


def _c_back_body(x_ref, oc_ref, os_ref, ow_ref, og_ref, wg_ref, wo_ref, lg_ref, lb_ref, o_ref):
    x = x_ref[...]
    gates = _sigmoid(_dot(x.astype(BF16), wg_ref[...]))
    o = gates[:, :1024] * oc_ref[...] + gates[:, 1024:2048] * os_ref[...] + gates[:, 2048:] * ow_ref[...]
    out = _dot((o * _silu(og_ref[...])).astype(BF16), wo_ref[...])
    o_ref[...] = _deepnorm_ln(x, out, lg_ref[...], lb_ref[...])


def _c_back(x, o_c, o_s, o_w, og, pc, ln_g, ln_b):
    (out,) = _tok_call(_c_back_body, x.shape[0], [x, o_c, o_s, o_w, og],
                       [pc["w_grep"], pc["wout"], ln_g.reshape(1, -1), ln_b.reshape(1, -1)], [(D_MODEL, F32)])
    return out


def _online_update(s, mask, v_fn, m_ref, l_ref, acc_ref):
    if mask is not None:
        s = jnp.where(mask, s, NEG)
    m_prev = m_ref[...]
    m_new = jnp.maximum(m_prev, jnp.max(s, axis=1, keepdims=True))
    p = jnp.exp(s - m_new)
    if mask is not None:
        p = jnp.where(mask, p, 0.0)
    corr = jnp.exp(m_prev - m_new)
    l_ref[...] = l_ref[...] * corr + jnp.sum(p, axis=1, keepdims=True)
    acc_ref[...] = acc_ref[...] * corr + v_fn(p.astype(BF16))
    m_ref[...] = m_new


LANES = 128


def _flash_step(s, v_fn, m_ref, l_ref, acc_ref):
    m_prev = m_ref[...]
    m_new = jnp.maximum(m_prev, jnp.max(s, axis=1, keepdims=True))
    p = jnp.exp(s - pltpu.repeat(m_new, s.shape[1] // LANES, axis=1))
    corr = jnp.exp(m_prev - m_new)
    l_ref[...] = l_ref[...] * corr + jnp.sum(p, axis=1, keepdims=True)
    dv = acc_ref.shape[-1]
    acc_ref[...] = acc_ref[...] * (corr if dv == LANES else pltpu.repeat(corr, dv // LANES, axis=1)) + v_fn(p.astype(BF16))
    m_ref[...] = m_new


def _flash_out(l_ref, acc_ref):
    dv = acc_ref.shape[-1]
    inv = 1.0 / l_ref[...]
    return acc_ref[...] * (inv if dv == LANES else pltpu.repeat(inv, dv // LANES, axis=1))


NEW_PAD = 16


def _pad_rows16(x):
    return jnp.concatenate([x, jnp.zeros((NEW_PAD - x.shape[0], x.shape[1]), x.dtype)], axis=0)


def _flash_init(m_ref, l_ref, acc_ref):
    m_ref[...] = jnp.full(m_ref.shape, NEG, F32)
    l_ref[...] = jnp.zeros(l_ref.shape, F32)
    acc_ref[...] = jnp.zeros(acc_ref.shape, F32)


def _flash_result(l_ref, acc_ref):
    l = l_ref[...]
    return acc_ref[...] / jnp.where(l > 0.0, l, 1.0)


def _causal_schedule(n_q, tq, tk):
    qi, ki = [], []
    for i in range(n_q):
        last = (i * tq + tq - 1) // tk
        for k in range(last + 1):
            qi.append(i)
            ki.append(k)
    return jnp.asarray(np.array(qi, np.int32)), jnp.asarray(np.array(ki, np.int32))


def _key_tile(seq, want):
    tk = want
    while seq % tk:
        tk //= 2
    return max(tk, TQ)


def _mla_prompt_body(qi_ref, ki_ref, ql_ref, qr_ref, lat_ref, kr_ref, o_ref, m_ref, l_ref, acc_ref, *, tk):
    step = pl.program_id(0)
    qi, ki = qi_ref[step], ki_ref[step]

    last = (qi * TQ + TQ - 1) // tk

    @pl.when(ki == 0)
    def _():
        _flash_init(m_ref, l_ref, acc_ref)

    def run(diagonal):
        lat = lat_ref[...]
        s = _dot_nt(ql_ref[...], lat) + _dot_nt(qr_ref[...], kr_ref[...])
        if diagonal:
            rows = s.shape[0]
            tok = qi * TQ - ki * tk + (lax.broadcasted_iota(jnp.int32, (rows, tk), 0) >> 3)
            s = jnp.where(lax.broadcasted_iota(jnp.int32, (rows, tk), 1) <= tok, s, NEG)
        _flash_step(s, lambda p: _dot(p, lat), m_ref, l_ref, acc_ref)

    @pl.when(ki < last)
    def _():
        run(False)

    @pl.when(ki == last)
    def _():
        run(True)
        o_ref[...] = _flash_out(l_ref, acc_ref).astype(o_ref.dtype)


def _mla_prompt(ql, qr, latb, krb):
    seq = latb.shape[0]
    tk = _key_tile(seq, 1024)
    rows = TQ * MLA_HEADS
    qi, ki = _causal_schedule(seq // TQ, TQ, tk)
    grid_spec = pltpu.PrefetchScalarGridSpec(
        num_scalar_prefetch=2, grid=(qi.shape[0],),
        in_specs=[pl.BlockSpec((rows, KV_RANK), lambda s, q, k: (q[s], 0)),
                  pl.BlockSpec((rows, ROPE), lambda s, q, k: (q[s], 0)),
                  pl.BlockSpec((tk, KV_RANK), lambda s, q, k: (k[s], 0)),
                  pl.BlockSpec((tk, ROPE), lambda s, q, k: (k[s], 0))],
        out_specs=pl.BlockSpec((rows, KV_RANK), lambda s, q, k: (q[s], 0)),
        scratch_shapes=[pltpu.VMEM((rows, LANES), F32), pltpu.VMEM((rows, LANES), F32),
                        pltpu.VMEM((rows, KV_RANK), F32)])
    return pl.pallas_call(
        functools.partial(_mla_prompt_body, tk=tk), grid_spec=grid_spec, name="mla_prompt",
        out_shape=jax.ShapeDtypeStruct((seq * MLA_HEADS, KV_RANK), BF16),
        compiler_params=_cparams(("arbitrary",)))(qi, ki, ql, qr, latb, krb)


def _pages_per_step(n_pages):
    for pp in (16, 8, 4, 2):
        if n_pages % pp == 0:
            return pp
    return 1


def _mla_sample_body(pt_ref, ql_ref, qr_ref, latn_ref, krn_ref, *refs, pp, n_new):
    lat_refs, kr_refs = refs[:pp], refs[pp:2 * pp]
    o_ref, m_ref, l_ref, acc_ref = refs[2 * pp:]
    pc = pl.program_id(1)

    @pl.when(pc == 0)
    def _():
        _flash_init(m_ref, l_ref, acc_ref)

    ql, qr = ql_ref[...], qr_ref[...]
    lats = [r[...].astype(BF16) for r in lat_refs]
    s = jnp.concatenate([_dot_nt(ql, lats[j]) + _dot(qr, kr_refs[j][...].astype(BF16)) for j in range(pp)], axis=1)

    def pv(p):
        acc = _dot(p[:, :PAGE], lats[0])
        for j in range(1, pp):
            acc = acc + _dot(p[:, j * PAGE:(j + 1) * PAGE], lats[j])
        return acc

    _online_update(s, None, pv, m_ref, l_ref, acc_ref)

    @pl.when(pc == pl.num_programs(1) - 1)
    def _():
        latn = _pad_rows16(latn_ref[...]).astype(BF16)
        sn = _dot_nt(ql, latn) + _dot_nt(qr, _pad_rows16(krn_ref[...]).astype(BF16))
        rows = sn.shape[0]
        tok = lax.broadcasted_iota(jnp.int32, (rows, NEW_PAD), 0) >> 3
        key = lax.broadcasted_iota(jnp.int32, (rows, NEW_PAD), 1)
        _online_update(sn, key <= tok, lambda p: _dot(p, latn), m_ref, l_ref, acc_ref)
        o_ref[...] = _flash_result(l_ref, acc_ref).astype(o_ref.dtype)


def _mla_sample(ql, qr, lat_new, kr_new, pool_lat, pool_kr, li, page_table, n_new):
    db, n_pages = page_table.shape
    pp = _pages_per_step(n_pages)
    rows = n_new * MLA_HEADS
    row_spec = lambda w: pl.BlockSpec((rows, w), lambda b, c, pt: (b, 0))
    new_spec = lambda w: pl.BlockSpec((n_new, w), lambda b, c, pt: (b, 0))

    def page_spec(shape, j):
        return pl.BlockSpec((None, None) + shape, lambda b, c, pt, _j=j: (li, pt[b, c * pp + _j], 0, 0))

    pool_kr = jnp.transpose(pool_kr, (0, 1, 3, 2))
    grid_spec = pltpu.PrefetchScalarGridSpec(
        num_scalar_prefetch=1, grid=(db, n_pages // pp),
        in_specs=[row_spec(KV_RANK), row_spec(ROPE), new_spec(KV_RANK), new_spec(ROPE)]
        + [page_spec((PAGE, KV_RANK), j) for j in range(pp)] + [page_spec((ROPE, PAGE), j) for j in range(pp)],
        out_specs=row_spec(KV_RANK),
        scratch_shapes=[pltpu.VMEM((rows, 1), F32), pltpu.VMEM((rows, 1), F32), pltpu.VMEM((rows, KV_RANK), F32)])
    return pl.pallas_call(
        functools.partial(_mla_sample_body, pp=pp, n_new=n_new), grid_spec=grid_spec, name="mla_sample",
        out_shape=jax.ShapeDtypeStruct((db * rows, KV_RANK), BF16),
        compiler_params=_cparams(("parallel", "arbitrary")))(
            page_table, ql, qr, lat_new, kr_new, *([pool_lat] * pp), *([pool_kr] * pp))


def _ssd_body(xbc_ref, dt_ref, dtt_ref, conv0_ref, h0_ref, cw_ref, cb_ref, alr_ref, alc_ref, dsk_ref,
              y_ref, convn_ref, hn_ref, xpad_ref, h_ref, *, q):
    c = pl.program_id(1)
    keep = CONV_W - 1

    @pl.when(c == 0)
    def _():
        h_ref[...] = h0_ref[...]
        xpad_ref[8 - keep:8, :] = conv0_ref[...]

    xpad_ref[8:8 + q, :] = xbc_ref[...]
    conv = cb_ref[...] + xpad_ref[8 - keep:8 - keep + q, :] * cw_ref[0:1, :]
    for w in range(1, CONV_W):
        conv = conv + xpad_ref[8 - keep + w:8 - keep + w + q, :] * cw_ref[w:w + 1, :]
    tail = xpad_ref[8 + q - keep:8 + q, :]
    convn_ref[...] = tail
    xpad_ref[8 - keep:8, :] = tail

    u = _silu(conv)
    xs = u[:, :D_INNER]
    gn = SSM_GROUPS * D_STATE
    bm, cm = u[:, D_INNER:D_INNER + gn], u[:, D_INNER + gn:]

    dt = dt_ref[...]
    dtt = dtt_ref[...]
    a_row = -jnp.exp(alr_ref[...])
    a_col = -jnp.exp(alc_ref[...])
    ii = lax.broadcasted_iota(jnp.int32, (q, q), 0)
    jj = lax.broadcasted_iota(jnp.int32, (q, q), 1)
    tril = ii >= jj
    cs = jnp.dot(tril.astype(F32), dt * a_row, precision=lax.Precision.HIGHEST, preferred_element_type=F32)
    cst = jnp.dot(dtt * a_col, (ii <= jj).astype(F32), precision=lax.Precision.HIGHEST, preferred_element_type=F32)
    rpg = SSM_HEADS // SSM_GROUPS
    for g in range(SSM_GROUPS):
        bg = bm[:, g * D_STATE:(g + 1) * D_STATE]
        cg = cm[:, g * D_STATE:(g + 1) * D_STATE].astype(BF16)
        cb = _dot_nt(cg, bg.astype(BF16))
        for r in range(rpg):
            h = g * rpg + r
            cs_col, cs_row = cs[:, h:h + 1], cst[h:h + 1, :]
            dt_col, dt_row = dt[:, h:h + 1], dtt[h:h + 1, :]
            cs_last = cst[h:h + 1, q - 1:q]
            decay = jnp.where(tril, jnp.exp(jnp.minimum(cs_col - cs_row, 0.0)), 0.0)
            lmat = (cb * decay * dt_row).astype(BF16)
            xs_h = xs[:, h * SSM_HEADDIM:(h + 1) * SSM_HEADDIM]
            xs_hb = xs_h.astype(BF16)
            y_diag = _dot(lmat, xs_hb)
            h_prev = h_ref[h]
            y_off = _dot_nt(cg, h_prev.astype(BF16)) * jnp.exp(cs_col)
            y_ref[:, h * SSM_HEADDIM:(h + 1) * SSM_HEADDIM] = y_diag + y_off + xs_h * dsk_ref[0:1, h:h + 1]
            bw = (bg * (jnp.exp(cs_last - cs_col) * dt_col)).astype(BF16)
            h_ref[h] = h_prev * jnp.exp(cs_last) + _dot_tn(xs_hb, bw)
    hn_ref[...] = h_ref[...]


def _ssd(xbc, dt, conv0, h0, conv_w, conv_b, a_log, d_skip, batch):
    rows = xbc.shape[0]
    length = rows // batch
    q = SSD_CHUNK if length % SSD_CHUNK == 0 else length
    nc = length // q
    dtt = jnp.transpose(dt.reshape(batch * nc, q, SSM_HEADS), (0, 2, 1))
    keep = CONV_W - 1
    in_specs = [pl.BlockSpec((q, CONV_DIM), lambda b, c: (b * nc + c, 0)),
                pl.BlockSpec((q, SSM_HEADS), lambda b, c: (b * nc + c, 0)),
                pl.BlockSpec((None, SSM_HEADS, q), lambda b, c: (b * nc + c, 0, 0)),
                pl.BlockSpec((None, keep, CONV_DIM), lambda b, c: (b, 0, 0)),
                pl.BlockSpec((None, SSM_HEADS, SSM_HEADDIM, D_STATE), lambda b, c: (b, 0, 0, 0)),
                _const_spec((CONV_W, CONV_DIM)), _const_spec((1, CONV_DIM)), _const_spec((1, SSM_HEADS)),
                _const_spec((SSM_HEADS, 1)), _const_spec((1, SSM_HEADS))]
    out_specs = [pl.BlockSpec((q, D_INNER), lambda b, c: (b * nc + c, 0)),
                 pl.BlockSpec((None, keep, CONV_DIM), lambda b, c: (b, 0, 0)),
                 pl.BlockSpec((None, SSM_HEADS, SSM_HEADDIM, D_STATE), lambda b, c: (b, 0, 0, 0))]
    out_shape = [jax.ShapeDtypeStruct((rows, D_INNER), F32), jax.ShapeDtypeStruct((batch, keep, CONV_DIM), F32),
                 jax.ShapeDtypeStruct((batch, SSM_HEADS, SSM_HEADDIM, D_STATE), F32)]
    return pl.pallas_call(
        functools.partial(_ssd_body, q=q), grid=(batch, nc), in_specs=in_specs, out_specs=out_specs,
        out_shape=out_shape, name="ssd",
        scratch_shapes=[pltpu.VMEM((8 + q, CONV_DIM), F32), pltpu.VMEM((SSM_HEADS, SSM_HEADDIM, D_STATE), F32)],
        compiler_params=_cparams(("parallel", "arbitrary")))(
            xbc, dt, dtt, conv0, h0, conv_w, conv_b.reshape(1, -1), a_log.reshape(1, -1), a_log.reshape(-1, 1),
            d_skip.reshape(1, -1))


CHUNK_W = CMP_STRIDE * NSA_GROUPS * NSA_D
KC_PAD = 16
NEAR = 32


def _prep_cmp(pe, w1, w2):
    eye = jnp.eye(NSA_GROUPS, dtype=w1.dtype)

    def big(w):
        return (w[:, None, :, None, :] * eye[None, :, None, :, None]).reshape(CHUNK_W, NSA_GROUPS * NSA_D)

    w_big = jnp.concatenate([big(w1[:CMP_STRIDE]), big(w1[CMP_STRIDE:])], axis=1).astype(BF16)
    peb = jnp.einsum('ld,ldh->h', pe, w1)
    peb2 = jnp.tile(peb, NSA_GROUPS).reshape(1, -1)
    w2_bd = (w2[None, :, None, :] * eye[:, None, :, None]).reshape(NSA_GROUPS * NSA_D, NSA_GROUPS * NSA_D).astype(BF16)
    return w_big, peb2, w2_bd


def _mm_body(x_ref, w_ref, o_ref):
    o_ref[...] = _dot(x_ref[...].astype(BF16), w_ref[...])


def _cmp_u_prompt(rows_kv, w_big):
    chunks = rows_kv.reshape(-1, CHUNK_W)
    (u,) = _tok_call(_mm_body, chunks.shape[0], [chunks], [w_big], [(w_big.shape[1], F32)])
    return u


def _feature_major(pool):
    return jnp.transpose(pool, (0, 1, 3, 4, 2))


def _cmp_u_paged_body(pt_ref, w_ref, *refs, pp):
    o_ref, tok_ref = refs[pp], refs[pp + 1]
    feat = NSA_GROUPS * NSA_D
    cpp = PAGE // CMP_STRIDE
    for j in range(pp):
        tok_ref[j * PAGE:(j + 1) * PAGE, :] = refs[j][...].reshape(feat, PAGE).T
    acc = None
    for l in range(CMP_STRIDE):
        x = tok_ref[pl.ds(l, pp * cpp, stride=CMP_STRIDE), :].astype(BF16)
        part = _dot(x, w_ref[l * feat:(l + 1) * feat, :])
        acc = part if acc is None else acc + part
    o_ref[...] = acc


def _cmp_u_paged(pool, li, page_table, w_big):
    db, n_pages = page_table.shape
    pp = _pages_per_step(n_pages)
    cpp = PAGE // CMP_STRIDE
    pool_t = _feature_major(pool)
    width = w_big.shape[1]
    grid_spec = pltpu.PrefetchScalarGridSpec(
        num_scalar_prefetch=1, grid=(db, n_pages // pp),
        in_specs=[pl.BlockSpec(w_big.shape, lambda b, c, pt: (0, 0))]
        + [pl.BlockSpec((None, None, NSA_GROUPS, NSA_D, PAGE),
                        lambda b, c, pt, _j=j: (li, pt[b, c * pp + _j], 0, 0, 0)) for j in range(pp)],
        out_specs=pl.BlockSpec((None, pp * cpp, width), lambda b, c, pt: (b, c, 0)),
        scratch_shapes=[pltpu.VMEM((pp * PAGE, NSA_GROUPS * NSA_D), F32)])
    return pl.pallas_call(
        functools.partial(_cmp_u_paged_body, pp=pp), grid_spec=grid_spec, name="cmp_u_paged",
        out_shape=jax.ShapeDtypeStruct((db, n_pages * cpp, width), F32),
        compiler_params=_cparams(("parallel", "arbitrary")))(page_table, w_big, *([pool_t] * pp))


def _cmp_finish_body(uk_ref, uv_ref, pk_ref, pv_ref, wk_ref, wv_ref, kc_ref, vc_ref, *, n):
    half = NSA_GROUPS * NSA_D

    def fin(u_ref, p_ref, w_ref, o_ref):
        u = u_ref[...]
        hid = u[:, :half] + pltpu.roll(u[:, half:], n - 1, 0) + p_ref[...]
        o_ref[...] = jnp.zeros(o_ref.shape, F32)
        o_ref[KC_PAD:KC_PAD + n, :] = _dot(_silu(hid).astype(BF16), w_ref[...])

    fin(uk_ref, pk_ref, wk_ref, kc_ref)
    fin(uv_ref, pv_ref, wv_ref, vc_ref)


def _cmp_finish(uk, uv, pk, pv, batch):
    n = uk.shape[0] // batch
    half = NSA_GROUPS * NSA_D
    n_out = KC_PAD + n + NEAR
    u_spec = pl.BlockSpec((n, 2 * half), lambda b: (b, 0))
    o_spec = pl.BlockSpec((None, n_out, half), lambda b: (b, 0, 0))
    consts = [pk[1], pv[1], pk[2], pv[2]]
    return pl.pallas_call(
        functools.partial(_cmp_finish_body, n=n), grid=(batch,), name="cmp_finish",
        in_specs=[u_spec, u_spec] + [_const_spec(a.shape) for a in consts], out_specs=[o_spec, o_spec],
        out_shape=[jax.ShapeDtypeStruct((batch, n_out, half), F32)] * 2,
        compiler_params=_cparams(("parallel",)))(uk, uv, *consts)


def _stack_heads(q, g):
    t = q.shape[0]
    lane = lax.broadcasted_iota(jnp.int32, (t, 128), 1)
    lo = lane < NSA_D
    parts = []
    for m in range(g * 4, g * 4 + 4):
        x = q[:, 128 * m:128 * (m + 1)]
        xr = pltpu.roll(x, NSA_D, 1)
        if g == 0:
            parts += [jnp.where(lo, x, 0.0), jnp.where(lo, xr, 0.0)]
        else:
            parts += [jnp.where(lo, 0.0, xr), jnp.where(lo, 0.0, x)]
    return jnp.concatenate(parts, axis=0)


def _unstack_heads(o, g, t):
    lane = lax.broadcasted_iota(jnp.int32, (t, 128), 1)
    lo = lane < NSA_D
    blocks = []
    for m in range(4):
        a, b = o[(2 * m) * t:(2 * m + 1) * t], o[(2 * m + 1) * t:(2 * m + 2) * t]
        if g == 0:
            blocks.append(jnp.where(lo, a, pltpu.roll(b, NSA_D, 1)))
        else:
            blocks.append(jnp.where(lo, pltpu.roll(a, NSA_D, 1), b))
    return blocks


def _bias_rows(rel_bias, dist):
    t, k = dist.shape
    b = rel_bias[jnp.asarray(_t5_bucket_np(dist))]
    return jnp.transpose(b, (2, 0, 1)).reshape(NSA_GROUPS, NSA_HPG * t, k).astype(F32)


def _far_bias(rel_bias, t):
    col = jnp.repeat(rel_bias[NUM_BUCKETS - 1], t)
    return col.reshape(NSA_GROUPS, NSA_HPG * t, 1).astype(F32)


def _topk_mask(score, lane_f, n_rounds):
    sel = jnp.zeros(score.shape, F32)
    for _ in range(n_rounds):
        mx = jnp.max(score, axis=1, keepdims=True)
        idx = jnp.min(jnp.where(score == mx, lane_f, 1e9), axis=1, keepdims=True)
        pick = lane_f == idx
        sel = jnp.where(pick, 1.0, sel)
        score = jnp.where(pick, -2.0, score)
    return sel


def _select_scores(imp, tpos, n_slc):
    lane = lax.broadcasted_iota(jnp.int32, imp.shape, 1)
    cur = tpos >> 6
    forced = (lane == 0) | (lane == cur) | (lane == cur - 1)
    score = jnp.where(lane <= cur, jnp.where(forced, 1e30, imp), -1.0)
    score = jnp.where(lane < n_slc, score, -10.0)
    return score, lane.astype(F32)


def _cmp_prompt_body(q_ref, kc_ref, vc_ref, ov_ref, bn_ref, bf_ref, oc_ref, sel_ref, *, n_cmp, n_slc):
    i = pl.program_id(0)
    q = q_ref[...]
    kc_all, vc_all = kc_ref[...].astype(BF16), vc_ref[...].astype(BF16)
    n_rows = kc_all.shape[0]
    start = pl.multiple_of(i * (TQ // CMP_STRIDE), 8)
    kc_near = kc_ref[pl.ds(start, NEAR), :].astype(BF16)
    vc_near = vc_ref[pl.ds(start, NEAR), :].astype(BF16)
    ov_all = ov_ref[...].astype(BF16)
    ov_near = ov_ref[pl.ds(start, NEAR), :].astype(BF16)
    rows = NSA_HPG * TQ
    colf = lax.broadcasted_iota(jnp.int32, (rows, n_rows), 1)
    mask_f = (colf >= KC_PAD) & (colf < start)
    t_n = lax.broadcasted_iota(jnp.int32, (rows, NEAR), 0) & (TQ - 1)
    jj = lax.broadcasted_iota(jnp.int32, (rows, NEAR), 1)
    jabs = start - KC_PAD + jj
    mask_n = (t_n - CMP_STRIDE * (jj - KC_PAD) - (CMP_LEN - 1) >= 0) & (jabs >= 0) & (jabs < n_cmp)
    tpos = i * TQ + lax.broadcasted_iota(jnp.int32, (TQ, 1), 0)
    for g in range(NSA_GROUPS):
        qg = _stack_heads(q, g).astype(BF16)
        s_f = jnp.where(mask_f, _dot_nt(qg, kc_all) + bf_ref[g], NEG)
        s_n = jnp.where(mask_n, _dot_nt(qg, kc_near) + bn_ref[g], NEG)
        m = jnp.maximum(jnp.max(s_f, axis=1, keepdims=True), jnp.max(s_n, axis=1, keepdims=True))
        p_f = jnp.where(mask_f, jnp.exp(s_f - m), 0.0)
        p_n = jnp.where(mask_n, jnp.exp(s_n - m), 0.0)
        l = jnp.sum(p_f, axis=1, keepdims=True) + jnp.sum(p_n, axis=1, keepdims=True)
        inv = 1.0 / jnp.where(l > 0.0, l, 1.0)
        p_f, p_n = p_f * inv, p_n * inv
        o = _dot(p_f.astype(BF16), vc_all) + _dot(p_n.astype(BF16), vc_near)
        for m4, blk in enumerate(_unstack_heads(o, g, TQ)):
            col = (g * 4 + m4) * 128
            oc_ref[:, col:col + 128] = blk
        ps_f, ps_n = p_f[:TQ], p_n[:TQ]
        for r in range(1, NSA_HPG):
            ps_f = ps_f + p_f[r * TQ:(r + 1) * TQ]
            ps_n = ps_n + p_n[r * TQ:(r + 1) * TQ]
        imp = _dot(ps_f.astype(BF16), ov_all) + _dot(ps_n.astype(BF16), ov_near)
        score, lane_f = _select_scores(imp, tpos, n_slc)
        picked = _topk_mask(score, lane_f, min(N_SEL, n_slc))
        sel_ref[g] = jnp.where(picked > 0.5, 0.0, NEG).astype(sel_ref.dtype)


def _overlap_np(n_rows, n_cmp, n_slc, width):
    ov = np.zeros((n_rows, width), np.float32)
    cs = np.arange(n_cmp)[:, None] * CMP_STRIDE
    bs = np.arange(n_slc)[None, :] * SLC_BLOCK
    ov[KC_PAD:KC_PAD + n_cmp, :n_slc] = np.maximum(
        np.minimum(cs + CMP_LEN, bs + SLC_BLOCK) - np.maximum(cs, bs), 0)
    return ov


def _cmp_prompt(q, kc, vc, rel_bias, seq):
    n_rows = kc.shape[0]
    n_cmp = (seq - CMP_LEN) // CMP_STRIDE + 1
    n_slc = -(-seq // SLC_BLOCK)
    w_slc = -(-n_slc // 128) * 128
    ov = jnp.asarray(_overlap_np(n_rows, n_cmp, n_slc, w_slc))
    t = np.arange(TQ)[:, None]
    jj = np.arange(NEAR)[None, :]
    b_near = _bias_rows(rel_bias, t - CMP_STRIDE * (jj - KC_PAD) - (CMP_LEN - 1))
    b_far = _far_bias(rel_bias, TQ)
    rows = NSA_HPG * TQ
    return pl.pallas_call(
        functools.partial(_cmp_prompt_body, n_cmp=n_cmp, n_slc=n_slc), grid=(seq // TQ,), name="cmp_prompt",
        in_specs=[pl.BlockSpec((TQ, 1024), lambda i: (i, 0)), _const_spec(kc.shape), _const_spec(vc.shape),
                  _const_spec(ov.shape), _const_spec(b_near.shape), _const_spec(b_far.shape)],
        out_specs=[pl.BlockSpec((TQ, 1024), lambda i: (i, 0)), pl.BlockSpec((NSA_GROUPS, TQ, w_slc), lambda i: (0, i, 0))],
        out_shape=[jax.ShapeDtypeStruct((seq, 1024), F32), jax.ShapeDtypeStruct((NSA_GROUPS, seq, w_slc), BF16)],
        compiler_params=_cparams(("parallel",)))(q, kc, vc, ov, b_near, b_far)


def _cmp_sample_body(q_ref, kc_ref, vc_ref, ov_ref, bias_ref, oc_ref, sel_ref, *, n_cmp, n_slc, past, t):
    q = q_ref[...]
    kc_all, vc_all = kc_ref[...].astype(BF16), vc_ref[...].astype(BF16)
    ov_all = ov_ref[...].astype(BF16)
    n_rows = kc_all.shape[0]
    rows = NSA_HPG * t
    col = lax.broadcasted_iota(jnp.int32, (rows, n_rows), 1)
    tok = lax.broadcasted_iota(jnp.int32, (rows, n_rows), 0) & (t - 1)
    j = col - KC_PAD
    mask = (j >= 0) & (j < n_cmp) & (past + tok - CMP_STRIDE * j - (CMP_LEN - 1) >= 0)
    tpos = past + lax.broadcasted_iota(jnp.int32, (t, 1), 0)
    for g in range(NSA_GROUPS):
        qg = _stack_heads(q, g).astype(BF16)
        s = jnp.where(mask, _dot_nt(qg, kc_all) + bias_ref[g], NEG)
        m = jnp.max(s, axis=1, keepdims=True)
        p = jnp.where(mask, jnp.exp(s - m), 0.0)
        l = jnp.sum(p, axis=1, keepdims=True)
        p = p * (1.0 / jnp.where(l > 0.0, l, 1.0))
        o = _dot(p.astype(BF16), vc_all)
        for m4, blk in enumerate(_unstack_heads(o, g, t)):
            c0 = (g * 4 + m4) * 128
            oc_ref[:, c0:c0 + 128] = blk
        ps = p[:t]
        for r in range(1, NSA_HPG):
            ps = ps + p[r * t:(r + 1) * t]
        imp = _dot(ps.astype(BF16), ov_all)
        score, lane_f = _select_scores(imp, tpos, n_slc)
        sel_ref[g] = _topk_mask(score, lane_f, min(N_SEL, n_slc)).astype(sel_ref.dtype)


def _cmp_sample(q, kc, vc, rel_bias, past, t):
    db, n_rows = kc.shape[0], kc.shape[1]
    total = past + t
    n_cmp = (total - CMP_LEN) // CMP_STRIDE + 1
    n_slc = -(-total // SLC_BLOCK)
    w_slc = -(-n_slc // 128) * 128
    ov = jnp.asarray(_overlap_np(n_rows, n_cmp, n_slc, w_slc))
    tt = np.arange(t)[:, None]
    jn = np.arange(n_rows)[None, :] - KC_PAD
    bias = _bias_rows(rel_bias, past + tt - CMP_STRIDE * jn - (CMP_LEN - 1))
    return pl.pallas_call(
        functools.partial(_cmp_sample_body, n_cmp=n_cmp, n_slc=n_slc, past=past, t=t), grid=(db,), name="cmp_sample",
        in_specs=[pl.BlockSpec((t, 1024), lambda b: (b, 0)), pl.BlockSpec((None, n_rows, 128), lambda b: (b, 0, 0)),
                  pl.BlockSpec((None, n_rows, 128), lambda b: (b, 0, 0)), _const_spec(ov.shape), _const_spec(bias.shape)],
        out_specs=[pl.BlockSpec((t, 1024), lambda b: (b, 0)),
                   pl.BlockSpec((None, NSA_GROUPS, t, w_slc), lambda b: (b, 0, 0, 0))],
        out_shape=[jax.ShapeDtypeStruct((db * t, 1024), F32), jax.ShapeDtypeStruct((db, NSA_GROUPS, t, w_slc), F32)],
        compiler_params=_cparams(("parallel",)))(q, kc, vc, ov, bias)


def _slc_prompt_body(qi_ref, ki_ref, q_ref, ks_ref, vs_ref, sel_ref, ex_ref, br_ref, o_ref,
                     qs_ref, m_ref, l_ref, acc_ref, *, tk):
    step = pl.program_id(0)
    qi, ki = qi_ref[step], ki_ref[step]
    rows = NSA_HPG * TQ
    sub = tk // TQ

    @pl.when(ki == 0)
    def _():
        q = q_ref[...]
        for g in range(NSA_GROUPS):
            qs_ref[g] = _stack_heads(q, g).astype(BF16)
            _flash_init(m_ref.at[g], l_ref.at[g], acc_ref.at[g])

    near = (ki * sub + sub - 1) >= qi - 1

    def run(is_near):
        ks, vs = ks_ref[...], vs_ref[...]
        for g in range(NSA_GROUPS):
            mb = _dot(sel_ref[g], ex_ref[...])
            if is_near:
                tok = qi * TQ - ki * tk + lax.broadcasted_iota(jnp.int32, (TQ, tk), 0)
                mb = jnp.where(lax.broadcasted_iota(jnp.int32, (TQ, tk), 1) <= tok, mb, NEG)
            s = _dot_nt(qs_ref[g], ks)
            s = (s.reshape(NSA_HPG, TQ, tk) + mb[None]).reshape(rows, tk)
            if is_near:
                pieces = []
                for u in range(sub):
                    delta = qi - (ki * sub + u)
                    pieces.append(jnp.where(delta == 0, br_ref[g, 0], jnp.where(delta == 1, br_ref[g, 1], 0.0)))
                s = s + jnp.concatenate(pieces, axis=1)
            _flash_step(s, lambda p: _dot(p, vs), m_ref.at[g], l_ref.at[g], acc_ref.at[g])

    @pl.when(near)
    def _():
        run(True)

    @pl.when(jnp.logical_not(near))
    def _():
        run(False)

    @pl.when(ki == (qi * TQ + TQ - 1) // tk)
    def _():
        for g in range(NSA_GROUPS):
            o = _flash_out(l_ref.at[g], acc_ref.at[g])
            for m4, blk in enumerate(_unstack_heads(o, g, TQ)):
                col = (g * 4 + m4) * 128
                o_ref[:, col:col + 128] = blk


def _tile_bias(rel_bias):
    t = np.arange(TQ)[:, None]
    c = np.arange(TQ)[None, :]
    return jnp.stack([_bias_rows(rel_bias, d * TQ + t - c) for d in (0, 1)], axis=1)


def _slc_prompt(q, ksb, vsb, sel_neg, rel_bias, seq):
    tk = _key_tile(seq, 1024)
    rows = NSA_HPG * TQ
    nk = seq // tk
    bpt = tk // SLC_BLOCK
    qi, ki = _causal_schedule(seq // TQ, TQ, tk)
    b_rel = _tile_bias(rel_bias) - _far_bias(rel_bias, TQ)[:, None]
    sel_t = jnp.transpose(sel_neg[:, :, :nk * bpt].reshape(NSA_GROUPS, seq, nk, bpt), (0, 2, 1, 3))
    expand = jnp.asarray((np.arange(bpt)[:, None] == np.arange(tk)[None, :] // SLC_BLOCK).astype(np.float32)).astype(BF16)
    grid_spec = pltpu.PrefetchScalarGridSpec(
        num_scalar_prefetch=2, grid=(qi.shape[0],),
        in_specs=[pl.BlockSpec((TQ, 1024), lambda s, a, b: (a[s], 0)),
                  pl.BlockSpec((tk, 128), lambda s, a, b: (b[s], 0)),
                  pl.BlockSpec((tk, 128), lambda s, a, b: (b[s], 0)),
                  pl.BlockSpec((NSA_GROUPS, None, TQ, bpt), lambda s, a, b: (0, b[s], a[s], 0)),
                  pl.BlockSpec(expand.shape, lambda s, a, b: (0, 0)),
                  pl.BlockSpec(b_rel.shape, lambda s, a, b: (0, 0, 0, 0))],
        out_specs=pl.BlockSpec((TQ, 1024), lambda s, a, b: (a[s], 0)),
        scratch_shapes=[pltpu.VMEM((NSA_GROUPS, rows, 128), BF16), pltpu.VMEM((NSA_GROUPS, rows, LANES), F32),
                        pltpu.VMEM((NSA_GROUPS, rows, LANES), F32), pltpu.VMEM((NSA_GROUPS, rows, 128), F32)])
    return pl.pallas_call(
        functools.partial(_slc_prompt_body, tk=tk), grid_spec=grid_spec, name="slc_prompt",
        out_shape=jax.ShapeDtypeStruct((seq, 1024), F32),
        compiler_params=_cparams(("arbitrary",)))(qi, ki, q, ksb, vsb, sel_t, expand, b_rel)


N_WIN_BLK = WINDOW // TQ + 1


def _win_prompt_body(q_ref, *refs):
    k_refs, v_refs = refs[:N_WIN_BLK], refs[N_WIN_BLK:2 * N_WIN_BLK]
    bn_ref, bf_ref, o_ref = refs[2 * N_WIN_BLK:]
    i = pl.program_id(0)
    q = q_ref[...]
    rows = NSA_HPG * TQ
    n_keys = N_WIN_BLK * TQ
    k = jnp.concatenate([r[...] for r in k_refs], axis=0)
    v = jnp.concatenate([r[...] for r in v_refs], axis=0)
    t = lax.broadcasted_iota(jnp.int32, (rows, n_keys), 0) & (TQ - 1)
    c = lax.broadcasted_iota(jnp.int32, (rows, n_keys), 1)
    dist = WINDOW + t - c
    mask = (dist >= 0) & (dist < WINDOW) & ((i - (N_WIN_BLK - 1)) * TQ + c >= 0)
    for g in range(NSA_GROUPS):
        qg = _stack_heads(q, g).astype(BF16)
        far = jnp.broadcast_to(bf_ref[g], (rows, n_keys - 2 * TQ))
        bias = jnp.concatenate([far, bn_ref[g, 1], bn_ref[g, 0]], axis=1)
        s = jnp.where(mask, _dot_nt(qg, k) + bias, NEG)
        m = jnp.max(s, axis=1, keepdims=True)
        p = jnp.exp(s - m)
        o = _dot(p.astype(BF16), v) / jnp.sum(p, axis=1, keepdims=True)
        for m4, blk in enumerate(_unstack_heads(o, g, TQ)):
            col = (g * 4 + m4) * 128
            o_ref[:, col:col + 128] = blk


def _win_prompt(q, kwb, vwb, rel_bias, seq):
    b_near = _tile_bias(rel_bias)
    b_far = _far_bias(rel_bias, TQ)

    def kv_spec(u):
        return pl.BlockSpec((TQ, 128), lambda i, _u=u: (jnp.maximum(i - (N_WIN_BLK - 1) + _u, 0), 0))

    return pl.pallas_call(
        _win_prompt_body, grid=(seq // TQ,), name="win_prompt",
        in_specs=[pl.BlockSpec((TQ, 1024), lambda i: (i, 0))] + [kv_spec(u) for u in range(N_WIN_BLK)] * 2
        + [_const_spec(b_near.shape), _const_spec(b_far.shape)],
        out_specs=pl.BlockSpec((TQ, 1024), lambda i: (i, 0)),
        out_shape=jax.ShapeDtypeStruct((seq, 1024), F32),
        compiler_params=_cparams(("parallel",)))(q, *([kwb] * N_WIN_BLK), *([vwb] * N_WIN_BLK), b_near, b_far)


def _slc_sample_body(pt_ref, q_ref, sel_ref, seln_ref, ex_ref, bias_ref, bnew_ref, kn_ref, vn_ref, *refs, pp, t):
    k_refs, v_refs = refs[:pp], refs[pp:2 * pp]
    o_ref, qs_ref, m_ref, l_ref, acc_ref = refs[2 * pp:]
    pc = pl.program_id(1)
    rows = NSA_HPG * t
    tk = pp * PAGE

    @pl.when(pc == 0)
    def _():
        q = q_ref[...]
        for g in range(NSA_GROUPS):
            qs_ref[g] = _stack_heads(q, g).astype(BF16)
            _flash_init(m_ref.at[g], l_ref.at[g], acc_ref.at[g])

    feat = NSA_GROUPS * NSA_D
    ks = [r[...].reshape(feat, PAGE).astype(BF16) for r in k_refs]
    vs = [r[...].reshape(feat, PAGE).astype(BF16) for r in v_refs]
    ex = ex_ref[...]
    for g in range(NSA_GROUPS):
        qg = qs_ref[g]
        s = jnp.concatenate([_dot(qg, ks[j]) for j in range(pp)], axis=1) + bias_ref[g]
        selx = _dot(sel_ref[g].astype(BF16), ex)
        mask = jnp.concatenate([selx] * NSA_HPG, axis=0) > 0.5

        def pv(p):
            acc = _dot_nt(p[:, :PAGE], vs[0])
            for j in range(1, pp):
                acc = acc + _dot_nt(p[:, j * PAGE:(j + 1) * PAGE], vs[j])
            return acc

        _online_update(s, mask, pv, m_ref.at[g], l_ref.at[g], acc_ref.at[g])

    @pl.when(pc == pl.num_programs(1) - 1)
    def _():
        kn, vn = _pad_rows16(kn_ref[...]).astype(BF16), _pad_rows16(vn_ref[...]).astype(BF16)
        tok = lax.broadcasted_iota(jnp.int32, (rows, NEW_PAD), 0) & (t - 1)
        key = lax.broadcasted_iota(jnp.int32, (rows, NEW_PAD), 1)
        for g in range(NSA_GROUPS):
            s = _dot_nt(qs_ref[g], kn) + bnew_ref[g]
            seln = jnp.concatenate([seln_ref[g]] * NSA_HPG, axis=0) > 0.5
            _online_update(s, seln & (key <= tok), lambda p: _dot(p, vn), m_ref.at[g], l_ref.at[g], acc_ref.at[g])
            o = _flash_result(l_ref.at[g], acc_ref.at[g])
            for m4, blk in enumerate(_unstack_heads(o, g, t)):
                col = (g * 4 + m4) * 128
                o_ref[:, col:col + 128] = blk


def _slc_sample(q, sel, ks_new, vs_new, pool_k, pool_v, li, page_table, rel_bias, t):
    db, n_pages = page_table.shape
    past = n_pages * PAGE
    pp = _pages_per_step(n_pages)
    n_steps = n_pages // pp
    tk = pp * PAGE
    bpt = tk // SLC_BLOCK
    rows = NSA_HPG * t
    sel_past = sel[..., :n_steps * bpt].reshape(db, NSA_GROUPS, t, n_steps, bpt)
    sel_past = jnp.transpose(sel_past, (0, 3, 1, 2, 4))
    new_blk = past // SLC_BLOCK
    sel_new = jnp.broadcast_to(sel[..., new_blk:new_blk + 1], (db, NSA_GROUPS, t, NEW_PAD))
    expand = jnp.asarray((np.arange(bpt)[:, None] == np.arange(tk)[None, :] // SLC_BLOCK).astype(np.float32)).astype(BF16)
    tt = np.arange(t)[:, None]
    assert tk >= FAR_DIST
    last_keys = (n_steps - 1) * tk + np.arange(tk)[None, :]
    bias = jnp.stack([jnp.broadcast_to(_far_bias(rel_bias, t), (NSA_GROUPS, rows, tk)),
                      _bias_rows(rel_bias, past + tt - last_keys)])
    b_new = _bias_rows(rel_bias, tt - np.arange(NEW_PAD)[None, :])
    pk, pv = _feature_major(pool_k), _feature_major(pool_v)

    def page_spec(j):
        return pl.BlockSpec((None, None, NSA_GROUPS, NSA_D, PAGE),
                            lambda b, c, pt, _j=j: (li, pt[b, c * pp + _j], 0, 0, 0))

    grid_spec = pltpu.PrefetchScalarGridSpec(
        num_scalar_prefetch=1, grid=(db, n_steps),
        in_specs=[pl.BlockSpec((t, 1024), lambda b, c, pt: (b, 0)),
                  pl.BlockSpec((None, None, NSA_GROUPS, t, bpt), lambda b, c, pt: (b, c, 0, 0, 0)),
                  pl.BlockSpec((None, NSA_GROUPS, t, NEW_PAD), lambda b, c, pt: (b, 0, 0, 0)),
                  pl.BlockSpec(expand.shape, lambda b, c, pt: (0, 0)),
                  pl.BlockSpec((None, NSA_GROUPS, rows, tk),
                               lambda b, c, pt: (jnp.where(c == n_steps - 1, 1, 0), 0, 0, 0)),
                  pl.BlockSpec(b_new.shape, lambda b, c, pt: (0, 0, 0)),
                  pl.BlockSpec((t, 128), lambda b, c, pt: (b, 0)),
                  pl.BlockSpec((t, 128), lambda b, c, pt: (b, 0))]
        + [page_spec(j) for j in range(pp)] * 2,
        out_specs=pl.BlockSpec((t, 1024), lambda b, c, pt: (b, 0)),
        scratch_shapes=[pltpu.VMEM((NSA_GROUPS, rows, 128), BF16), pltpu.VMEM((NSA_GROUPS, rows, 1), F32),
                        pltpu.VMEM((NSA_GROUPS, rows, 1), F32), pltpu.VMEM((NSA_GROUPS, rows, 128), F32)])
    return pl.pallas_call(
        functools.partial(_slc_sample_body, pp=pp, t=t), grid_spec=grid_spec, name="slc_sample",
        out_shape=jax.ShapeDtypeStruct((db * t, 1024), F32),
        compiler_params=_cparams(("parallel", "arbitrary")))(
            page_table, q, sel_past, sel_new, expand, bias, b_new, ks_new, vs_new, *([pk] * pp), *([pv] * pp))


def _win_sample_body(q_ref, wk_ref, wv_ref, kn_ref, vn_ref, bias_ref, o_ref, *, t, w_buf):
    q = q_ref[...]
    rows = NSA_HPG * t
    feat = NSA_GROUPS * NSA_D
    kt = wk_ref[...].reshape(feat, w_buf).astype(BF16)
    vt = wv_ref[...].reshape(feat, w_buf).astype(BF16)
    kn, vn = _pad_rows16(kn_ref[...]).astype(BF16), _pad_rows16(vn_ref[...]).astype(BF16)
    n_keys = w_buf + NEW_PAD
    tok = lax.broadcasted_iota(jnp.int32, (rows, n_keys), 0) & (t - 1)
    c = lax.broadcasted_iota(jnp.int32, (rows, n_keys), 1)
    dist = w_buf + tok - c
    mask = (dist >= 0) & (dist < WINDOW)
    for g in range(NSA_GROUPS):
        qg = _stack_heads(q, g).astype(BF16)
        s = jnp.concatenate([_dot(qg, kt), _dot_nt(qg, kn)], axis=1) + bias_ref[g]
        s = jnp.where(mask, s, NEG)
        m = jnp.max(s, axis=1, keepdims=True)
        p = jnp.exp(s - m)
        pb = p.astype(BF16)
        o = (_dot_nt(pb[:, :w_buf], vt) + _dot(pb[:, w_buf:], vn)) / jnp.sum(p, axis=1, keepdims=True)
        for m4, blk in enumerate(_unstack_heads(o, g, t)):
            col = (g * 4 + m4) * 128
            o_ref[:, col:col + 128] = blk


def _win_sample(q, wk_buf, wv_buf, kw_new, vw_new, rel_bias, t):
    db, w_buf = wk_buf.shape[0], wk_buf.shape[1]
    tt = np.arange(t)[:, None]
    bias = _bias_rows(rel_bias, w_buf + tt - np.arange(w_buf + NEW_PAD)[None, :])
    wk = jnp.transpose(wk_buf, (0, 2, 3, 1))
    wv = jnp.transpose(wv_buf, (0, 2, 3, 1))
    buf_spec = pl.BlockSpec((None, NSA_GROUPS, NSA_D, w_buf), lambda b: (b, 0, 0, 0))
    return pl.pallas_call(
        functools.partial(_win_sample_body, t=t, w_buf=w_buf), grid=(db,), name="win_sample",
        in_specs=[pl.BlockSpec((t, 1024), lambda b: (b, 0)), buf_spec, buf_spec, pl.BlockSpec((t, 128), lambda b: (b, 0)),
                  pl.BlockSpec((t, 128), lambda b: (b, 0)), _const_spec(bias.shape)],
        out_specs=pl.BlockSpec((t, 1024), lambda b: (b, 0)),
        out_shape=jax.ShapeDtypeStruct((db * t, 1024), F32),
        compiler_params=_cparams(("parallel",)))(q, wk, wv, kw_new, vw_new, bias)


def kernel(x_prompt, x_sample, cache_mla_latent, cache_mla_krope, state_ssm, state_conv, cache_cmp_k, cache_cmp_v, cache_slc_k, cache_slc_v, state_win_k, state_win_v, page_table, rel_bias, w_in_a, q_norm, w_uq, kv_norm, w_uk, w_uv, conv_w, conv_b, dt_bias, a_log, d_skip, ssm_norm, w_out_a, ln_a_g, ln_a_b, w_in_c, cmp_pe_k, cmp_w1_k, cmp_w2_k, cmp_pe_v, cmp_w1_v, cmp_w2_v, w_out_c, ln_c_g, ln_c_b):
    bp, sp = x_prompt.shape[:2]
    db, ss = x_sample.shape[:2]
    n_pages = page_table.shape[1]
    past = n_pages * PAGE
    assert bp == 1 and x_prompt.shape[2] == D_MODEL and cache_mla_latent.shape[2] == PAGE
    assert sp % TQ == 0 and sp >= WINDOW and ss == 8
    assert past % SLC_BLOCK == 0 and ss < CMP_STRIDE and past % CMP_STRIDE == 0 and past >= WINDOW
    assert state_win_k.shape[2] == WINDOW

    hp = x_prompt.reshape(sp, D_MODEL)
    hs = x_sample.reshape(db * ss, D_MODEL)
    pos_p = jnp.arange(sp)
    pos_s = jnp.tile(past + jnp.arange(ss), db)
    st = {}

    pa = _prep_a(w_in_a[0], q_norm[0], w_uq[0], kv_norm[0], w_uk[0], w_uv[0], dt_bias[0], w_out_a[0])
    qlat, qrope, lat, latb, kr, krb, g, z, xbc, dt = _a_front(hp, pos_p, pa)
    o_lat = _mla_prompt(qlat.reshape(sp * MLA_HEADS, KV_RANK), qrope.reshape(sp * MLA_HEADS, ROPE), latb, krb)
    y, cbuf, hst = _ssd(xbc, dt, jnp.zeros((1, CONV_W - 1, CONV_DIM), F32),
                        jnp.zeros((1, SSM_HEADS, SSM_HEADDIM, D_STATE), F32), conv_w[0], conv_b[0], a_log[0],
                        d_skip[0], 1)
    hp = _a_back(hp, o_lat.reshape(sp, MLA_HEADS * KV_RANK), g, y, z, pa, ssm_norm[0], ln_a_g[0], ln_a_b[0])
    st['p_lat'], st['p_krope'] = lat.reshape(1, 1, sp, KV_RANK), kr.reshape(1, 1, sp, ROPE)
    st['p_ssm'], st['p_conv'] = hst[None], cbuf[None]
    qlat, qrope, lat, latb, kr, krb, g, z, xbc, dt = _a_front(hs, pos_s, pa)
    o_lat = _mla_sample(qlat.reshape(db * ss * MLA_HEADS, KV_RANK), qrope.reshape(db * ss * MLA_HEADS, ROPE), lat, kr,
                        cache_mla_latent, cache_mla_krope, 0, page_table, ss)
    y, cbuf, hst = _ssd(xbc, dt, state_conv[0], state_ssm[0], conv_w[0], conv_b[0], a_log[0], d_skip[0], db)
    hs = _a_back(hs, o_lat.reshape(db * ss, MLA_HEADS * KV_RANK), g, y, z, pa, ssm_norm[0], ln_a_g[0], ln_a_b[0])
    st['s_lat'], st['s_krope'] = lat.reshape(1, db, ss, KV_RANK), kr.reshape(1, db, ss, ROPE)
    st['s_ssm'], st['s_conv'] = hst[None], cbuf[None]

    pc = _prep_c(w_in_c[0], w_out_c[0])
    pk = _prep_cmp(cmp_pe_k[0], cmp_w1_k[0], cmp_w2_k[0])
    pv = _prep_cmp(cmp_pe_v[0], cmp_w1_v[0], cmp_w2_v[0])
    kv5 = lambda a, b_, s_: a.reshape(1, b_, s_, NSA_GROUPS, NSA_D)
    q, kc_r, vc_r, ks, vs, kw, vw, ksb, vsb, kwb, vwb, og = _c_front(hp, pc)
    kc, vc = _cmp_finish(_cmp_u_prompt(kc_r, pk[0]), _cmp_u_prompt(vc_r, pv[0]), pk, pv, 1)
    o_c, sel = _cmp_prompt(q, kc[0], vc[0], rel_bias, sp)
    o_s = _slc_prompt(q, ksb, vsb, sel, rel_bias, sp)
    o_w = _win_prompt(q, kwb, vwb, rel_bias, sp)
    hp = _c_back(hp, o_c, o_s, o_w, og, pc, ln_c_g[0], ln_c_b[0])
    keep = min(WINDOW, sp)
    for n, a in (('p_cmp_k', kc_r), ('p_cmp_v', vc_r), ('p_slc_k', ks), ('p_slc_v', vs)):
        st[n] = kv5(a, 1, sp)
    st['p_win_k'], st['p_win_v'] = kv5(kw[sp - keep:], 1, keep), kv5(vw[sp - keep:], 1, keep)
    q, kc_r, vc_r, ks, vs, kw, vw, ksb, vsb, kwb, vwb, og = _c_front(hs, pc)
    uk = _cmp_u_paged(cache_cmp_k, 0, page_table, pk[0])
    uv = _cmp_u_paged(cache_cmp_v, 0, page_table, pv[0])
    n_chunks = uk.shape[1]
    kc, vc = _cmp_finish(uk.reshape(db * n_chunks, -1), uv.reshape(db * n_chunks, -1), pk, pv, db)
    o_c, sel = _cmp_sample(q, kc, vc, rel_bias, past, ss)
    o_s = _slc_sample(q, sel, ks, vs, cache_slc_k, cache_slc_v, 0, page_table, rel_bias, ss)
    o_w = _win_sample(q, state_win_k[0], state_win_v[0], kw, vw, rel_bias, ss)
    hs = _c_back(hs, o_c, o_s, o_w, og, pc, ln_c_g[0], ln_c_b[0])
    for n, a in (('s_cmp_k', kc_r), ('s_cmp_v', vc_r), ('s_slc_k', ks), ('s_slc_v', vs)):
        st[n] = kv5(a, db, ss)
    new5 = lambda a: a.reshape(db, ss, NSA_GROUPS, NSA_D)
    st['s_win_k'] = jnp.concatenate([state_win_k[0], new5(kw)], 1)[None, :, ss:]
    st['s_win_v'] = jnp.concatenate([state_win_v[0], new5(vw)], 1)[None, :, ss:]

    names = ['p_lat', 'p_krope', 'p_ssm', 'p_conv', 'p_cmp_k', 'p_cmp_v', 'p_slc_k', 'p_slc_v', 'p_win_k', 'p_win_v',
             's_lat', 's_krope', 's_ssm', 's_conv', 's_cmp_k', 's_cmp_v', 's_slc_k', 's_slc_v', 's_win_k', 's_win_v']
    return (hp.reshape(bp, sp, D_MODEL), hs.reshape(db, ss, D_MODEL)) + tuple(st[n] for n in names)
```

```python
import functools
import math

import numpy as np
import jax
import jax.numpy as jnp
from jax import lax
from jax.experimental import pallas as pl
from jax.experimental.pallas import tpu as pltpu

F32 = jnp.float32
BF16 = jnp.bfloat16

D_MODEL = 1024
DEPTH = 2
DEEPNORM_ALPHA = (2.0 * DEPTH) ** 0.25
LN_EPS = 1e-5
RMS_EPS = 1e-6
NEG = -1e30
PAGE = 128

MLA_HEADS = 8
Q_RANK = 384
KV_RANK = 256
NOPE = 64
ROPE = 32
MLA_V = 64
MLA_WIDTH = MLA_HEADS * MLA_V
LOG2E = math.log2(math.e)
MLA_SCALE = (NOPE + ROPE) ** -0.5 * LOG2E
ROPE_BASE = 10000.0

SSM_HEADDIM = 64
D_INNER = 1024
SSM_HEADS = 16
SSM_GROUPS = 2
D_STATE = 128
CONV_W = 4
CONV_DIM = D_INNER + 2 * SSM_GROUPS * D_STATE
SSD_CHUNK = 128

NSA_HEADS = 16
NSA_GROUPS = 2
NSA_HPG = 8
NSA_D = 64
NSA_SCALE = NSA_D ** -0.5 * LOG2E
CMP_LEN = 32
CMP_STRIDE = 16
SLC_BLOCK = 64
N_SEL = 16
WINDOW = 512
NUM_BUCKETS = 32
MAX_DISTANCE = 128
FAR_DIST = 128

TQ = 128
VMEM_LIMIT = 56 * 1024 * 1024


def _cparams(sem):
    return pltpu.CompilerParams(dimension_semantics=sem, vmem_limit_bytes=VMEM_LIMIT)


def _const_spec(shape):
    nd = len(shape)
    return pl.BlockSpec(shape, lambda *a, _nd=nd: (0,) * _nd)


def _dot(a, b):
    return jnp.dot(a, b, preferred_element_type=F32)


def _dot_nt(a, b):
    return lax.dot_general(a, b, (((1,), (1,)), ((), ())), preferred_element_type=F32)


def _dot_tn(a, b):
    return lax.dot_general(a, b, (((0,), (0,)), ((), ())), preferred_element_type=F32)


def _sigmoid(x):
    return 1.0 / (1.0 + jnp.exp(-x))


def _silu(x):
    return x * _sigmoid(x)


def _rms(x, g):
    return x * lax.rsqrt(jnp.mean(x * x, axis=-1, keepdims=True) + RMS_EPS) * g


def _deepnorm_ln(x, out, g, b):
    h = DEEPNORM_ALPHA * x + out
    mu = jnp.mean(h, axis=-1, keepdims=True)
    d = h - mu
    var = jnp.mean(d * d, axis=-1, keepdims=True)
    return d * lax.rsqrt(var + LN_EPS) * g + b


def _t5_bucket_np(dist):
    n = np.maximum(dist, 0)
    exact = NUM_BUCKETS // 2
    nf = np.maximum(n, exact).astype(np.float32)
    large = exact + (np.log(nf / np.float32(exact)) / np.float32(math.log(MAX_DISTANCE / exact))
                     * np.float32(NUM_BUCKETS - exact)).astype(np.int32)
    return np.where(n < exact, n, np.minimum(large, NUM_BUCKETS - 1)).astype(np.int32)


def _row_tile(rows):
    return 256 if rows % 256 == 0 else rows


def _tok_call(body, rows, row_ins, const_ins, outs):
    tm = _row_tile(rows)
    in_specs = [pl.BlockSpec((tm, a.shape[1]), lambda i: (i, 0)) for a in row_ins]
    in_specs += [_const_spec(a.shape) for a in const_ins]
    out_specs = [pl.BlockSpec((tm, w), lambda i: (i, 0)) for w, _ in outs]
    out_shape = [jax.ShapeDtypeStruct((rows, w), dt) for w, dt in outs]
    return pl.pallas_call(
        body, grid=(rows // tm,), in_specs=in_specs, out_specs=out_specs, out_shape=out_shape,
        name=body.__name__.strip("_").removesuffix("_body"), compiler_params=_cparams(("parallel",)))(*row_ins, *const_ins)


A_CQ, A_CKV, A_G, A_Z, A_XBC, A_MISC, A_END = 0, 384, 640, 1152, 2176, 3712, 3840


def _a_front_body(x_ref, ccq_ref, ssq_ref, cck_ref, ssk_ref, w_ref, qn_ref, kvn_ref, wuq_ref, wuk_ref, dtb_ref,
                  qlat_ref, qrope_ref, lat_ref, latb_ref, kr_ref, krb_ref, g_ref, z_ref, xbc_ref, dt_ref):
    xb = x_ref[...].astype(BF16)

    def proj(lo, hi):
        return _dot(xb, w_ref[:, lo:hi])

    cqn = _rms(proj(A_CQ, A_CKV), qn_ref[...])
    qall = _dot(cqn.astype(BF16), wuq_ref[...])
    qrope = (qall[:, 512:768] * ccq_ref[...] + qall[:, 768:1024] * ssq_ref[...]) * MLA_SCALE
    qrope_ref[...] = qrope.astype(BF16)
    qlat = _dot(qall[:, :512].astype(BF16), wuk_ref[...]) * MLA_SCALE
    qlat_ref[...] = qlat.astype(BF16)
    lat = _rms(proj(A_CKV, A_G), kvn_ref[...])
    lat_ref[...] = lat
    latb_ref[...] = lat.astype(BF16)
    g_ref[...] = proj(A_G, A_Z)
    z_ref[...] = proj(A_Z, A_XBC)
    xbc_ref[...] = proj(A_XBC, A_MISC)
    misc = proj(A_MISC, A_END)
    kr = misc[:, 0:32] * cck_ref[...] + misc[:, 32:64] * ssk_ref[...]
    kr_ref[...] = kr
    krb_ref[...] = kr.astype(BF16)
    v = misc[:, 64:80] + dtb_ref[...]
    dt_ref[...] = jnp.maximum(v, 0.0) + jnp.log1p(jnp.exp(-jnp.abs(v)))


def _rot_cols(w):
    h = ROPE // 2
    return jnp.concatenate([-w[..., h:], w[..., :h]], axis=-1)


def _prep_a(w_in, q_norm, w_uq, kv_norm, w_uk, w_uv, dt_bias, w_out):
    cq, ckv, kr, g, z, xbc, dtw = jnp.split(w_in, np.cumsum([Q_RANK, KV_RANK, ROPE, MLA_WIDTH, D_INNER, CONV_DIM])[:].tolist(),
                                            axis=1)
    pad = jnp.zeros((w_in.shape[0], A_END - A_MISC - 2 * ROPE - SSM_HEADS), w_in.dtype)
    w1 = jnp.concatenate([cq, ckv, g, z, xbc, kr, _rot_cols(kr), dtw, pad], axis=1).astype(BF16)
    wq = w_uq.reshape(Q_RANK, MLA_HEADS, NOPE + ROPE)
    wq_n = wq[:, :, :NOPE].reshape(Q_RANK, MLA_HEADS * NOPE)
    wq_r = wq[:, :, NOPE:]
    wuq = jnp.concatenate([wq_n, wq_r.reshape(Q_RANK, -1), _rot_cols(wq_r).reshape(Q_RANK, -1)], axis=1).astype(BF16)
    eye = jnp.eye(MLA_HEADS, dtype=w_uk.dtype)
    wuk = (jnp.transpose(w_uk, (1, 2, 0))[:, :, None, :] * eye[:, None, :, None]).reshape(
        MLA_HEADS * NOPE, MLA_HEADS * KV_RANK).astype(BF16)
    wuv = (jnp.transpose(w_uv, (1, 0, 2))[:, :, None, :] * eye[:, None, :, None]).reshape(
        MLA_HEADS * KV_RANK, MLA_WIDTH).astype(BF16)
    return dict(w1=w1, qn=q_norm.reshape(1, -1), kvn=kv_norm.reshape(1, -1), wuq=wuq, wuk=wuk, wuv=wuv,
                dtb=dt_bias.reshape(1, -1), wout=w_out.astype(BF16))


def _rope_tables(pos):
    half = ROPE // 2
    inv = ROPE_BASE ** (-jnp.arange(half, dtype=F32) / half)
    ang = pos.astype(F32)[:, None] * inv[None, :]
    c, s = jnp.cos(ang), jnp.sin(ang)
    cck, ssk = jnp.concatenate([c, c], 1), jnp.concatenate([s, s], 1)
    return jnp.tile(cck, (1, MLA_HEADS)), jnp.tile(ssk, (1, MLA_HEADS)), cck, ssk


def _a_front(x, pos, pa):
    rows = x.shape[0]
    ccq, ssq, cck, ssk = _rope_tables(pos)
    outs = [(MLA_HEADS * KV_RANK, BF16), (MLA_HEADS * ROPE, BF16), (KV_RANK, F32), (KV_RANK, BF16), (ROPE, F32),
            (ROPE, BF16), (MLA_WIDTH, F32), (D_INNER, F32), (CONV_DIM, F32), (SSM_HEADS, F32)]
    return _tok_call(_a_front_body, rows, [x, ccq, ssq, cck, ssk],
                     [pa["w1"], pa["qn"], pa["kvn"], pa["wuq"], pa["wuk"], pa["dtb"]], outs)


def _a_back_body(x_ref, ol_ref, g_ref, y_ref, z_ref, wuv_ref, sn_ref, wo_ref, lg_ref, lb_ref, o_ref):
    o_mla = _dot(ol_ref[...], wuv_ref[...])
    a = o_mla * _silu(g_ref[...])
    yn = _rms(y_ref[...] * _silu(z_ref[...]), sn_ref[...])
    out = _dot(a.astype(BF16), wo_ref[:MLA_WIDTH, :]) + _dot(yn.astype(BF16), wo_ref[MLA_WIDTH:, :])
    o_ref[...] = _deepnorm_ln(x_ref[...], out, lg_ref[...], lb_ref[...])


def _a_back(x, o_lat, g, y, z, pa, ssm_norm, ln_g, ln_b):
    (out,) = _tok_call(_a_back_body, x.shape[0], [x, o_lat, g, y, z],
                       [pa["wuv"], ssm_norm.reshape(1, -1), pa["wout"], ln_g.reshape(1, -1), ln_b.reshape(1, -1)],
                       [(D_MODEL, F32)])
    return out


C_Q, C_KV, C_OG, C_END = 0, 1024, 1792, 2816


def _c_front_body(x_ref, w_ref, q_ref, kc_ref, vc_ref, ks_ref, vs_ref, kw_ref, vw_ref, ksb_ref, vsb_ref, kwb_ref,
                  vwb_ref, og_ref):
    xb = x_ref[...].astype(BF16)
    q_ref[...] = _dot(xb, w_ref[:, C_Q:C_KV]) * NSA_SCALE
    kv = _dot(xb, w_ref[:, C_KV:C_OG])
    for j, r in enumerate((kc_ref, vc_ref, ks_ref, vs_ref, kw_ref, vw_ref)):
        r[...] = kv[:, 128 * j:128 * (j + 1)]
    for j, r in enumerate((ksb_ref, vsb_ref, kwb_ref, vwb_ref)):
        r[...] = kv[:, 128 * (j + 2):128 * (j + 3)].astype(BF16)
    og_ref[...] = _dot(xb, w_ref[:, C_OG:C_END])


def _prep_c(w_in, w_out):
    n_kv = 6 * NSA_GROUPS * NSA_D
    w_main = jnp.concatenate([w_in[:, :1024 + n_kv], w_in[:, 1024 + n_kv + 3 * NSA_HEADS:]], axis=1).astype(BF16)
    w_g = w_in[:, 1024 + n_kv:1024 + n_kv + 3 * NSA_HEADS]
    w_grep = jnp.repeat(w_g, NSA_D, axis=1).astype(BF16)
    return dict(w_main=w_main, w_grep=w_grep, wout=w_out.astype(BF16))


def _c_front(x, pc):
    outs = [(1024, F32)] + [(128, F32)] * 6 + [(128, BF16)] * 4 + [(1024, F32)]
    return _tok_call(_c_front_body, x.shape[0], [x], [pc["w_main"]], outs)


def _c_back_body(x_ref, oc_ref, os_ref, ow_ref, og_ref, wg_ref, wo_ref, lg_ref, lb_ref, o_ref):
    x = x_ref[...]
    gates = _sigmoid(_dot(x.astype(BF16), wg_ref[...]))
    o = gates[:, :1024] * oc_ref[...] + gates[:, 1024:2048] * os_ref[...] + gates[:, 2048:] * ow_ref[...]
    out = _dot((o * _silu(og_ref[...])).astype(BF16), wo_ref[...])
    o_ref[...] = _deepnorm_ln(x, out, lg_ref[...], lb_ref[...])


def _c_back(x, o_c, o_s, o_w, og, pc, ln_g, ln_b):
    (out,) = _tok_call(_c_back_body, x.shape[0], [x, o_c, o_s, o_w, og],
                       [pc["w_grep"], pc["wout"], ln_g.reshape(1, -1), ln_b.reshape(1, -1)], [(D_MODEL, F32)])
    return out


def _online_update(s, mask, v_fn, m_ref, l_ref, acc_ref):
    if mask is not None:
        s = jnp.where(mask, s, NEG)
    m_prev = m_ref[...]
    m_new = jnp.maximum(m_prev, jnp.max(s, axis=1, keepdims=True))
    p = jnp.exp2(s - m_new)
    if mask is not None:
        p = jnp.where(mask, p, 0.0)
    corr = jnp.exp2(m_prev - m_new)
    l_ref[...] = l_ref[...] * corr + jnp.sum(p, axis=1, keepdims=True)
    acc_ref[...] = acc_ref[...] * corr + v_fn(p.astype(BF16))
    m_ref[...] = m_new


LANES = 128


def _lane_tile(x, n):
    return x if n == 1 else jnp.concatenate([x] * n, axis=1)


def _flash_step(s, v_fn, m_ref, l_ref, acc_ref):
    m_prev = m_ref[...]
    m_new = jnp.maximum(m_prev, jnp.max(s, axis=1, keepdims=True))
    p = jnp.exp2(s - _lane_tile(m_new, s.shape[1] // LANES))
    corr = jnp.exp2(m_prev - m_new)
    l_ref[...] = l_ref[...] * corr + jnp.sum(p, axis=1, keepdims=True)
    dv = acc_ref.shape[-1]
    acc_ref[...] = acc_ref[...] * _lane_tile(corr, dv // LANES) + v_fn(p.astype(BF16))
    m_ref[...] = m_new


def _flash_out(l_ref, acc_ref):
    dv = acc_ref.shape[-1]
    inv = 1.0 / l_ref[...]
    return acc_ref[...] * _lane_tile(inv, dv // LANES)


NEW_PAD = 16


def _pad_rows16(x):
    return jnp.concatenate([x, jnp.zeros((NEW_PAD - x.shape[0], x.shape[1]), x.dtype)], axis=0)


def _flash_init(m_ref, l_ref, acc_ref):
    m_ref[...] = jnp.full(m_ref.shape, NEG, F32)
    l_ref[...] = jnp.zeros(l_ref.shape, F32)
    acc_ref[...] = jnp.zeros(acc_ref.shape, F32)


def _flash_result(l_ref, acc_ref):
    l = l_ref[...]
    return acc_ref[...] / jnp.where(l > 0.0, l, 1.0)


def _causal_schedule(n_q, tq, tk):
    qi, ki = [], []
    for i in range(n_q):
        last = (i * tq + tq - 1) // tk
        for k in range(last + 1):
            qi.append(i)
            ki.append(k)
    return jnp.asarray(np.array(qi, np.int32)), jnp.asarray(np.array(ki, np.int32))


def _key_tile(seq, want):
    tk = want
    while seq % tk:
        tk //= 2
    return max(tk, TQ)


def _mla_prompt_body(qi_ref, ki_ref, ql_ref, qr_ref, lat_ref, kr_ref, o_ref, m_ref, l_ref, acc_ref, *, tk):
    step = pl.program_id(0)
    qi, ki = qi_ref[step], ki_ref[step]

    last = (qi * TQ + TQ - 1) // tk

    @pl.when(ki == 0)
    def _():
        _flash_init(m_ref, l_ref, acc_ref)

    def run(diagonal):
        lat = lat_ref[...]
        s = _dot_nt(ql_ref[...], lat) + _dot_nt(qr_ref[...], kr_ref[...])
        if diagonal:
            rows = s.shape[0]
            tok = qi * TQ - ki * tk + (lax.broadcasted_iota(jnp.int32, (rows, tk), 0) >> 3)
            s = jnp.where(lax.broadcasted_iota(jnp.int32, (rows, tk), 1) <= tok, s, NEG)
        _flash_step(s, lambda p: _dot(p, lat), m_ref, l_ref, acc_ref)

    @pl.when(ki < last)
    def _():
        run(False)

    @pl.when(ki == last)
    def _():
        run(True)
        o_ref[...] = _flash_out(l_ref, acc_ref).astype(o_ref.dtype)


def _mla_prompt(ql, qr, latb, krb):
    seq = latb.shape[0]
    tk = _key_tile(seq, 1024)
    rows = TQ * MLA_HEADS
    qi, ki = _causal_schedule(seq // TQ, TQ, tk)
    grid_spec = pltpu.PrefetchScalarGridSpec(
        num_scalar_prefetch=2, grid=(qi.shape[0],),
        in_specs=[pl.BlockSpec((rows, KV_RANK), lambda s, q, k: (q[s], 0)),
                  pl.BlockSpec((rows, ROPE), lambda s, q, k: (q[s], 0)),
                  pl.BlockSpec((tk, KV_RANK), lambda s, q, k: (k[s], 0)),
                  pl.BlockSpec((tk, ROPE), lambda s, q, k: (k[s], 0))],
        out_specs=pl.BlockSpec((rows, KV_RANK), lambda s, q, k: (q[s], 0)),
        scratch_shapes=[pltpu.VMEM((rows, LANES), F32), pltpu.VMEM((rows, LANES), F32),
                        pltpu.VMEM((rows, KV_RANK), F32)])
    return pl.pallas_call(
        functools.partial(_mla_prompt_body, tk=tk), grid_spec=grid_spec, name="mla_prompt",
        out_shape=jax.ShapeDtypeStruct((seq * MLA_HEADS, KV_RANK), BF16),
        compiler_params=_cparams(("arbitrary",)))(qi, ki, ql, qr, latb, krb)


def _pages_per_step(n_pages):
    for pp in (32, 16, 8, 4, 2):
        if n_pages % pp == 0:
            return pp
    return 1


def _page_copy(pool_ref, buf_ref, sem_ref, li, page, slot, j):
    return pltpu.make_async_copy(pool_ref.at[li, page], buf_ref.at[slot, j], sem_ref.at[slot])


def _paged_pipeline(pt_ref, pools, bufs, sems, li, pp):
    b, c = pl.program_id(0), pl.program_id(1)
    nb, nc = pl.num_programs(0), pl.num_programs(1)
    step = b * nc + c
    slot = step % 2

    def fetch(b_, c_, slot_):
        for j in range(pp):
            page = pt_ref[b_, c_ * pp + j]
            for pool, buf, sem in zip(pools, bufs, sems):
                _page_copy(pool, buf, sem, li, page, slot_, j).start()

    @pl.when(step == 0)
    def _():
        fetch(0, 0, 0)

    @pl.when(step + 1 < nb * nc)
    def _():
        wrap = c + 1 == nc
        fetch(jnp.where(wrap, b + 1, b), jnp.where(wrap, 0, c + 1), 1 - slot)

    for j in range(pp):
        for pool, buf, sem in zip(pools, bufs, sems):
            _page_copy(pool, buf, sem, li, 0, slot, j).wait()
    return slot


def _mla_sample_body(pt_ref, ql_ref, qr_ref, latn_ref, krn_ref, lat_hbm, kr_hbm, o_ref,
                     latbuf, krbuf, sems, m_ref, l_ref, acc_ref, *, pp, li):
    slot = _paged_pipeline(pt_ref, (lat_hbm, kr_hbm), (latbuf, krbuf), (sems.at[0], sems.at[1]), li, pp)
    pc = pl.program_id(1)

    @pl.when(pc == 0)
    def _():
        _flash_init(m_ref, l_ref, acc_ref)

    ql, qr = ql_ref[...], qr_ref[...]
    lats = [latbuf[slot, j].astype(BF16) for j in range(pp)]
    s = jnp.concatenate([_dot_nt(ql, lats[j]) + _dot(qr, krbuf[slot, j].astype(BF16)) for j in range(pp)], axis=1)

    def pv(p):
        acc = _dot(p[:, :PAGE], lats[0])
        for j in range(1, pp):
            acc = acc + _dot(p[:, j * PAGE:(j + 1) * PAGE], lats[j])
        return acc

    _online_update(s, None, pv, m_ref, l_ref, acc_ref)

    @pl.when(pc == pl.num_programs(1) - 1)
    def _():
        latn = _pad_rows16(latn_ref[...]).astype(BF16)
        sn = _dot_nt(ql, latn) + _dot_nt(qr, _pad_rows16(krn_ref[...]).astype(BF16))
        rows = sn.shape[0]
        tok = lax.broadcasted_iota(jnp.int32, (rows, NEW_PAD), 0) >> 3
        key = lax.broadcasted_iota(jnp.int32, (rows, NEW_PAD), 1)
        _online_update(sn, key <= tok, lambda p: _dot(p, latn), m_ref, l_ref, acc_ref)
        o_ref[...] = _flash_result(l_ref, acc_ref).astype(o_ref.dtype)


def _mla_sample(ql, qr, lat_new, kr_new, pool_lat, pool_kr, li, page_table, n_new):
    db, n_pages = page_table.shape
    pp = _pages_per_step(n_pages)
    rows = n_new * MLA_HEADS
    row_spec = lambda w: pl.BlockSpec((rows, w), lambda b, c, pt: (b, 0))
    new_spec = lambda w: pl.BlockSpec((n_new, w), lambda b, c, pt: (b, 0))
    hbm_spec = pl.BlockSpec(memory_space=pl.ANY)
    pool_kr = jnp.transpose(pool_kr, (0, 1, 3, 2))
    grid_spec = pltpu.PrefetchScalarGridSpec(
        num_scalar_prefetch=1, grid=(db, n_pages // pp),
        in_specs=[row_spec(KV_RANK), row_spec(ROPE), new_spec(KV_RANK), new_spec(ROPE), hbm_spec, hbm_spec],
        out_specs=row_spec(KV_RANK),
        scratch_shapes=[pltpu.VMEM((2, pp, PAGE, KV_RANK), F32), pltpu.VMEM((2, pp, ROPE, PAGE), F32),
                        pltpu.SemaphoreType.DMA((2, 2)),
                        pltpu.VMEM((rows, 1), F32), pltpu.VMEM((rows, 1), F32), pltpu.VMEM((rows, KV_RANK), F32)])
    return pl.pallas_call(
        functools.partial(_mla_sample_body, pp=pp, li=li), grid_spec=grid_spec, name="mla_sample",
        out_shape=jax.ShapeDtypeStruct((db * rows, KV_RANK), BF16),
        compiler_params=_cparams(("arbitrary", "arbitrary")))(
            page_table, ql, qr, lat_new, kr_new, pool_lat, pool_kr)


def _ssd_body(xbc_ref, dt_ref, dtt_ref, conv0_ref, h0_ref, cw_ref, cb_ref, alr_ref, alc_ref, dsk_ref,
              y_ref, convn_ref, hn_ref, xpad_ref, h_ref, *, q):
    c = pl.program_id(1)
    keep = CONV_W - 1

    @pl.when(c == 0)
    def _():
        h_ref[...] = h0_ref[...]
        xpad_ref[8 - keep:8, :] = conv0_ref[...]

    xpad_ref[8:8 + q, :] = xbc_ref[...]
    conv = cb_ref[...] + xpad_ref[8 - keep:8 - keep + q, :] * cw_ref[0:1, :]
    for w in range(1, CONV_W):
        conv = conv + xpad_ref[8 - keep + w:8 - keep + w + q, :] * cw_ref[w:w + 1, :]
    tail = xpad_ref[8 + q - keep:8 + q, :]
    convn_ref[...] = tail
    xpad_ref[8 - keep:8, :] = tail

    u = _silu(conv)
    xs = u[:, :D_INNER]
    gn = SSM_GROUPS * D_STATE
    bm, cm = u[:, D_INNER:D_INNER + gn], u[:, D_INNER + gn:]

    dt = dt_ref[...]
    dtt = dtt_ref[...]
    a_row = -jnp.exp(alr_ref[...])
    a_col = -jnp.exp(alc_ref[...])
    ii = lax.broadcasted_iota(jnp.int32, (q, q), 0)
    jj = lax.broadcasted_iota(jnp.int32, (q, q), 1)
    tril = ii >= jj
    cs = jnp.dot(tril.astype(F32), dt * a_row, precision=lax.Precision.HIGHEST, preferred_element_type=F32)
    cst = jnp.dot(dtt * a_col, (ii <= jj).astype(F32), precision=lax.Precision.HIGHEST, preferred_element_type=F32)
    rpg = SSM_HEADS // SSM_GROUPS
    for g in range(SSM_GROUPS):
        bg = bm[:, g * D_STATE:(g + 1) * D_STATE]
        cg = cm[:, g * D_STATE:(g + 1) * D_STATE].astype(BF16)
        cb = _dot_nt(cg, bg.astype(BF16))
        for r in range(rpg):
            h = g * rpg + r
            cs_col, cs_row = cs[:, h:h + 1], cst[h:h + 1, :]
            dt_col, dt_row = dt[:, h:h + 1], dtt[h:h + 1, :]
            cs_last = cst[h:h + 1, q - 1:q]
            decay = jnp.where(tril, jnp.exp(jnp.minimum(cs_col - cs_row, 0.0)), 0.0)
            lmat = (cb * decay * dt_row).astype(BF16)
            xs_h = xs[:, h * SSM_HEADDIM:(h + 1) * SSM_HEADDIM]
            xs_hb = xs_h.astype(BF16)
            y_diag = _dot(lmat, xs_hb)
            h_prev = h_ref[h]
            y_off = _dot_nt(cg, h_prev.astype(BF16)) * jnp.exp(cs_col)
            y_ref[:, h * SSM_HEADDIM:(h + 1) * SSM_HEADDIM] = y_diag + y_off + xs_h * dsk_ref[0:1, h:h + 1]
            bw = (bg * (jnp.exp(cs_last - cs_col) * dt_col)).astype(BF16)
            h_ref[h] = h_prev * jnp.exp(cs_last) + _dot_tn(xs_hb, bw)
    hn_ref[...] = h_ref[...]


def _ssd(xbc, dt, conv0, h0, conv_w, conv_b, a_log, d_skip, batch):
    rows = xbc.shape[0]
    length = rows // batch
    q = SSD_CHUNK if length % SSD_CHUNK == 0 else length
    nc = length // q
    dtt = jnp.transpose(dt.reshape(batch * nc, q, SSM_HEADS), (0, 2, 1))
    keep = CONV_W - 1
    in_specs = [pl.BlockSpec((q, CONV_DIM), lambda b, c: (b * nc + c, 0)),
                pl.BlockSpec((q, SSM_HEADS), lambda b, c: (b * nc + c, 0)),
                pl.BlockSpec((None, SSM_HEADS, q), lambda b, c: (b * nc + c, 0, 0)),
                pl.BlockSpec((None, keep, CONV_DIM), lambda b, c: (b, 0, 0)),
                pl.BlockSpec((None, SSM_HEADS, SSM_HEADDIM, D_STATE), lambda b, c: (b, 0, 0, 0)),
                _const_spec((CONV_W, CONV_DIM)), _const_spec((1, CONV_DIM)), _const_spec((1, SSM_HEADS)),
                _const_spec((SSM_HEADS, 1)), _const_spec((1, SSM_HEADS))]
    out_specs = [pl.BlockSpec((q, D_INNER), lambda b, c: (b * nc + c, 0)),
                 pl.BlockSpec((None, keep, CONV_DIM), lambda b, c: (b, 0, 0)),
                 pl.BlockSpec((None, SSM_HEADS, SSM_HEADDIM, D_STATE), lambda b, c: (b, 0, 0, 0))]
    out_shape = [jax.ShapeDtypeStruct((rows, D_INNER), F32), jax.ShapeDtypeStruct((batch, keep, CONV_DIM), F32),
                 jax.ShapeDtypeStruct((batch, SSM_HEADS, SSM_HEADDIM, D_STATE), F32)]
    return pl.pallas_call(
        functools.partial(_ssd_body, q=q), grid=(batch, nc), in_specs=in_specs, out_specs=out_specs,
        out_shape=out_shape, name="ssd",
        scratch_shapes=[pltpu.VMEM((8 + q, CONV_DIM), F32), pltpu.VMEM((SSM_HEADS, SSM_HEADDIM, D_STATE), F32)],
        compiler_params=_cparams(("parallel", "arbitrary")))(
            xbc, dt, dtt, conv0, h0, conv_w, conv_b.reshape(1, -1), a_log.reshape(1, -1), a_log.reshape(-1, 1),
            d_skip.reshape(1, -1))


CHUNK_W = CMP_STRIDE * NSA_GROUPS * NSA_D
KC_PAD = 16
NEAR = 32


def _prep_cmp(pe, w1, w2):
    eye = jnp.eye(NSA_GROUPS, dtype=w1.dtype)

    def big(w):
        return (w[:, None, :, None, :] * eye[None, :, None, :, None]).reshape(CHUNK_W, NSA_GROUPS * NSA_D)

    w_big = jnp.concatenate([big(w1[:CMP_STRIDE]), big(w1[CMP_STRIDE:])], axis=1).astype(BF16)
    peb = jnp.einsum('ld,ldh->h', pe, w1)
    peb2 = jnp.tile(peb, NSA_GROUPS).reshape(1, -1)
    w2_bd = (w2[None, :, None, :] * eye[:, None, :, None]).reshape(NSA_GROUPS * NSA_D, NSA_GROUPS * NSA_D).astype(BF16)
    return w_big, peb2, w2_bd


def _mm_body(x_ref, w_ref, o_ref):
    o_ref[...] = _dot(x_ref[...].astype(BF16), w_ref[...])


def _cmp_u_prompt(rows_kv, w_big):
    chunks = rows_kv.reshape(-1, CHUNK_W)
    (u,) = _tok_call(_mm_body, chunks.shape[0], [chunks], [w_big], [(w_big.shape[1], F32)])
    return u


def _feature_major(pool):
    return jnp.transpose(pool, (0, 1, 3, 4, 2))


def _cmp_u_paged_body(pt_ref, wk_ref, wv_ref, k_hbm, v_hbm, uk_ref, uv_ref, kbuf, vbuf, sems, tok_ref, *, pp, li):
    slot = _paged_pipeline(pt_ref, (k_hbm, v_hbm), (kbuf, vbuf), (sems.at[0], sems.at[1]), li, pp)
    feat = NSA_GROUPS * NSA_D
    cpp = PAGE // CMP_STRIDE
    for buf, w_ref, o_ref in ((kbuf, wk_ref, uk_ref), (vbuf, wv_ref, uv_ref)):
        for j in range(pp):
            tok_ref[j * PAGE:(j + 1) * PAGE, :] = buf[slot, j].reshape(feat, PAGE).T
        acc = None
        for l in range(CMP_STRIDE):
            x = tok_ref[pl.ds(l, pp * cpp, stride=CMP_STRIDE), :].astype(BF16)
            part = _dot(x, w_ref[l * feat:(l + 1) * feat, :])
            acc = part if acc is None else acc + part
        o_ref[...] = acc


def _cmp_u_paged(pool_k, pool_v, li, page_table, wk_big, wv_big):
    db, n_pages = page_table.shape
    pp = _pages_per_step(n_pages)
    cpp = PAGE // CMP_STRIDE
    width = wk_big.shape[1]
    hbm_spec = pl.BlockSpec(memory_space=pl.ANY)
    w_spec = pl.BlockSpec(wk_big.shape, lambda b, c, pt: (0, 0))
    u_spec = pl.BlockSpec((None, pp * cpp, width), lambda b, c, pt: (b, c, 0))
    page_buf = pltpu.VMEM((2, pp, NSA_GROUPS, NSA_D, PAGE), F32)
    grid_spec = pltpu.PrefetchScalarGridSpec(
        num_scalar_prefetch=1, grid=(db, n_pages // pp),
        in_specs=[w_spec, w_spec, hbm_spec, hbm_spec], out_specs=[u_spec, u_spec],
        scratch_shapes=[page_buf, page_buf, pltpu.SemaphoreType.DMA((2, 2)),
                        pltpu.VMEM((pp * PAGE, NSA_GROUPS * NSA_D), F32)])
    u_shape = jax.ShapeDtypeStruct((db, n_pages * cpp, width), F32)
    return pl.pallas_call(
        functools.partial(_cmp_u_paged_body, pp=pp, li=li), grid_spec=grid_spec, name="cmp_u_paged",
        out_shape=[u_shape, u_shape],
        compiler_params=_cparams(("arbitrary", "arbitrary")))(
            page_table, wk_big, wv_big, _feature_major(pool_k), _feature_major(pool_v))


def _cmp_finish_body(uk_ref, uv_ref, pk_ref, pv_ref, wk_ref, wv_ref, kc_ref, vc_ref, *, n):
    half = NSA_GROUPS * NSA_D

    def fin(u_ref, p_ref, w_ref, o_ref):
        u = u_ref[...]
        hid = u[:, :half] + pltpu.roll(u[:, half:], n - 1, 0) + p_ref[...]
        o_ref[...] = jnp.zeros(o_ref.shape, F32)
        o_ref[KC_PAD:KC_PAD + n, :] = _dot(_silu(hid).astype(BF16), w_ref[...])

    fin(uk_ref, pk_ref, wk_ref, kc_ref)
    fin(uv_ref, pv_ref, wv_ref, vc_ref)


def _cmp_finish(uk, uv, pk, pv, batch):
    n = uk.shape[0] // batch
    half = NSA_GROUPS * NSA_D
    n_out = KC_PAD + n + NEAR
    u_spec = pl.BlockSpec((n, 2 * half), lambda b: (b, 0))
    o_spec = pl.BlockSpec((None, n_out, half), lambda b: (b, 0, 0))
    consts = [pk[1], pv[1], pk[2], pv[2]]
    return pl.pallas_call(
        functools.partial(_cmp_finish_body, n=n), grid=(batch,), name="cmp_finish",
        in_specs=[u_spec, u_spec] + [_const_spec(a.shape) for a in consts], out_specs=[o_spec, o_spec],
        out_shape=[jax.ShapeDtypeStruct((batch, n_out, half), F32)] * 2,
        compiler_params=_cparams(("parallel",)))(uk, uv, *consts)


def _stack_heads(q, g):
    t = q.shape[0]
    lane = lax.broadcasted_iota(jnp.int32, (t, 128), 1)
    lo = lane < NSA_D
    parts = []
    for m in range(g * 4, g * 4 + 4):
        x = q[:, 128 * m:128 * (m + 1)]
        xr = pltpu.roll(x, NSA_D, 1)
        if g == 0:
            parts += [jnp.where(lo, x, 0.0), jnp.where(lo, xr, 0.0)]
        else:
            parts += [jnp.where(lo, 0.0, xr), jnp.where(lo, 0.0, x)]
    return jnp.concatenate(parts, axis=0)


def _unstack_heads(o, g, t):
    lane = lax.broadcasted_iota(jnp.int32, (t, 128), 1)
    lo = lane < NSA_D
    blocks = []
    for m in range(4):
        a, b = o[(2 * m) * t:(2 * m + 1) * t], o[(2 * m + 1) * t:(2 * m + 2) * t]
        if g == 0:
            blocks.append(jnp.where(lo, a, pltpu.roll(b, NSA_D, 1)))
        else:
            blocks.append(jnp.where(lo, pltpu.roll(a, NSA_D, 1), b))
    return blocks


def _bias_rows(rel_bias, dist):
    t, k = dist.shape
    b = rel_bias[jnp.asarray(_t5_bucket_np(dist))] * LOG2E
    return jnp.transpose(b, (2, 0, 1)).reshape(NSA_GROUPS, NSA_HPG * t, k).astype(F32)


def _far_bias(rel_bias, t):
    col = jnp.repeat(rel_bias[NUM_BUCKETS - 1] * LOG2E, t)
    return col.reshape(NSA_GROUPS, NSA_HPG * t, 1).astype(F32)


def _topk_mask(score, lane_f, n_rounds):
    sel = jnp.zeros(score.shape, F32)
    for _ in range(n_rounds):
        mx = jnp.max(score, axis=1, keepdims=True)
        idx = jnp.min(jnp.where(score == mx, lane_f, 1e9), axis=1, keepdims=True)
        pick = lane_f == idx
        sel = jnp.where(pick, 1.0, sel)
        score = jnp.where(pick, -2.0, score)
    return sel


def _select_scores(imp, tpos, n_slc):
    lane = lax.broadcasted_iota(jnp.int32, imp.shape, 1)
    cur = tpos >> 6
    forced = (lane == 0) | (lane == cur) | (lane == cur - 1)
    score = jnp.where(lane <= cur, jnp.where(forced, 1e30, imp), -1.0)
    score = jnp.where(lane < n_slc, score, -10.0)
    return score, lane.astype(F32)


def _cmp_prompt_body(q_ref, kc_ref, vc_ref, ov_ref, bn_ref, bf_ref, oc_ref, sel_ref, *, n_cmp, n_slc):
    i = pl.program_id(0)
    q = q_ref[...]
    kc_all, vc_all = kc_ref[...].astype(BF16), vc_ref[...].astype(BF16)
    n_rows = kc_all.shape[0]
    start = pl.multiple_of(i * (TQ // CMP_STRIDE), 8)
    kc_near = kc_ref[pl.ds(start, NEAR), :].astype(BF16)
    vc_near = vc_ref[pl.ds(start, NEAR), :].astype(BF16)
    ov_all = ov_ref[...].astype(BF16)
    ov_near = ov_ref[pl.ds(start, NEAR), :].astype(BF16)
    rows = NSA_HPG * TQ
    colf = lax.broadcasted_iota(jnp.int32, (rows, n_rows), 1)
    mask_f = (colf >= KC_PAD) & (colf < start)
    t_n = lax.broadcasted_iota(jnp.int32, (rows, NEAR), 0) & (TQ - 1)
    jj = lax.broadcasted_iota(jnp.int32, (rows, NEAR), 1)
    jabs = start - KC_PAD + jj
    mask_n = (t_n - CMP_STRIDE * (jj - KC_PAD) - (CMP_LEN - 1) >= 0) & (jabs >= 0) & (jabs < n_cmp)
    tpos = i * TQ + lax.broadcasted_iota(jnp.int32, (TQ, 1), 0)
    for g in range(NSA_GROUPS):
        qg = _stack_heads(q, g).astype(BF16)
        s_f = jnp.where(mask_f, _dot_nt(qg, kc_all) + bf_ref[g], NEG)
        s_n = jnp.where(mask_n, _dot_nt(qg, kc_near) + bn_ref[g], NEG)
        m = jnp.maximum(jnp.max(s_f, axis=1, keepdims=True), jnp.max(s_n, axis=1, keepdims=True))
        p_f = jnp.where(mask_f, jnp.exp2(s_f - m), 0.0)
        p_n = jnp.where(mask_n, jnp.exp2(s_n - m), 0.0)
        l = jnp.sum(p_f, axis=1, keepdims=True) + jnp.sum(p_n, axis=1, keepdims=True)
        inv = 1.0 / jnp.where(l > 0.0, l, 1.0)
        p_f, p_n = p_f * inv, p_n * inv
        o = _dot(p_f.astype(BF16), vc_all) + _dot(p_n.astype(BF16), vc_near)
        for m4, blk in enumerate(_unstack_heads(o, g, TQ)):
            col = (g * 4 + m4) * 128
            oc_ref[:, col:col + 128] = blk
        ps_f, ps_n = p_f[:TQ], p_n[:TQ]
        for r in range(1, NSA_HPG):
            ps_f = ps_f + p_f[r * TQ:(r + 1) * TQ]
            ps_n = ps_n + p_n[r * TQ:(r + 1) * TQ]
        imp = _dot(ps_f.astype(BF16), ov_all) + _dot(ps_n.astype(BF16), ov_near)
        score, lane_f = _select_scores(imp, tpos, n_slc)
        picked = _topk_mask(score, lane_f, min(N_SEL, n_slc))
        sel_ref[g] = jnp.where(picked > 0.5, 0.0, NEG).astype(sel_ref.dtype)


def _overlap_np(n_rows, n_cmp, n_slc, width):
    ov = np.zeros((n_rows, width), np.float32)
    cs = np.arange(n_cmp)[:, None] * CMP_STRIDE
    bs = np.arange(n_slc)[None, :] * SLC_BLOCK
    ov[KC_PAD:KC_PAD + n_cmp, :n_slc] = np.maximum(
        np.minimum(cs + CMP_LEN, bs + SLC_BLOCK) - np.maximum(cs, bs), 0)
    return ov


def _cmp_prompt(q, kc, vc, rel_bias, seq):
    n_rows = kc.shape[0]
    n_cmp = (seq - CMP_LEN) // CMP_STRIDE + 1
    n_slc = -(-seq // SLC_BLOCK)
    w_slc = -(-n_slc // 128) * 128
    ov = jnp.asarray(_overlap_np(n_rows, n_cmp, n_slc, w_slc))
    t = np.arange(TQ)[:, None]
    jj = np.arange(NEAR)[None, :]
    b_near = _bias_rows(rel_bias, t - CMP_STRIDE * (jj - KC_PAD) - (CMP_LEN - 1))
    b_far = _far_bias(rel_bias, TQ)
    rows = NSA_HPG * TQ
    return pl.pallas_call(
        functools.partial(_cmp_prompt_body, n_cmp=n_cmp, n_slc=n_slc), grid=(seq // TQ,), name="cmp_prompt",
        in_specs=[pl.BlockSpec((TQ, 1024), lambda i: (i, 0)), _const_spec(kc.shape), _const_spec(vc.shape),
                  _const_spec(ov.shape), _const_spec(b_near.shape), _const_spec(b_far.shape)],
        out_specs=[pl.BlockSpec((TQ, 1024), lambda i: (i, 0)), pl.BlockSpec((NSA_GROUPS, TQ, w_slc), lambda i: (0, i, 0))],
        out_shape=[jax.ShapeDtypeStruct((seq, 1024), F32), jax.ShapeDtypeStruct((NSA_GROUPS, seq, w_slc), BF16)],
        compiler_params=_cparams(("parallel",)))(q, kc, vc, ov, b_near, b_far)


def _cmp_sample_body(q_ref, kc_ref, vc_ref, ov_ref, bias_ref, oc_ref, sel_ref, *, n_cmp, n_slc, past, t):
    q = q_ref[...]
    kc_all, vc_all = kc_ref[...].astype(BF16), vc_ref[...].astype(BF16)
    ov_all = ov_ref[...].astype(BF16)
    n_rows = kc_all.shape[0]
    rows = NSA_HPG * t
    col = lax.broadcasted_iota(jnp.int32, (rows, n_rows), 1)
    tok = lax.broadcasted_iota(jnp.int32, (rows, n_rows), 0) & (t - 1)
    j = col - KC_PAD
    mask = (j >= 0) & (j < n_cmp) & (past + tok - CMP_STRIDE * j - (CMP_LEN - 1) >= 0)
    tpos = past + lax.broadcasted_iota(jnp.int32, (t, 1), 0)
    for g in range(NSA_GROUPS):
        qg = _stack_heads(q, g).astype(BF16)
        s = jnp.where(mask, _dot_nt(qg, kc_all) + bias_ref[g], NEG)
        m = jnp.max(s, axis=1, keepdims=True)
        p = jnp.where(mask, jnp.exp2(s - m), 0.0)
        l = jnp.sum(p, axis=1, keepdims=True)
        p = p * (1.0 / jnp.where(l > 0.0, l, 1.0))
        o = _dot(p.astype(BF16), vc_all)
        for m4, blk in enumerate(_unstack_heads(o, g, t)):
            c0 = (g * 4 + m4) * 128
            oc_ref[:, c0:c0 + 128] = blk
        ps = p[:t]
        for r in range(1, NSA_HPG):
            ps = ps + p[r * t:(r + 1) * t]
        imp = _dot(ps.astype(BF16), ov_all)
        score, lane_f = _select_scores(imp, tpos, n_slc)
        sel_ref[g] = _topk_mask(score, lane_f, min(N_SEL, n_slc)).astype(sel_ref.dtype)


def _cmp_sample(q, kc, vc, rel_bias, past, t):
    db, n_rows = kc.shape[0], kc.shape[1]
    total = past + t
    n_cmp = (total - CMP_LEN) // CMP_STRIDE + 1
    n_slc = -(-total // SLC_BLOCK)
    w_slc = -(-n_slc // 128) * 128
    ov = jnp.asarray(_overlap_np(n_rows, n_cmp, n_slc, w_slc))
    tt = np.arange(t)[:, None]
    jn = np.arange(n_rows)[None, :] - KC_PAD
    bias = _bias_rows(rel_bias, past + tt - CMP_STRIDE * jn - (CMP_LEN - 1))
    return pl.pallas_call(
        functools.partial(_cmp_sample_body, n_cmp=n_cmp, n_slc=n_slc, past=past, t=t), grid=(db,), name="cmp_sample",
        in_specs=[pl.BlockSpec((t, 1024), lambda b: (b, 0)), pl.BlockSpec((None, n_rows, 128), lambda b: (b, 0, 0)),
                  pl.BlockSpec((None, n_rows, 128), lambda b: (b, 0, 0)), _const_spec(ov.shape), _const_spec(bias.shape)],
        out_specs=[pl.BlockSpec((t, 1024), lambda b: (b, 0)),
                   pl.BlockSpec((None, NSA_GROUPS, t, w_slc), lambda b: (b, 0, 0, 0))],
        out_shape=[jax.ShapeDtypeStruct((db * t, 1024), F32), jax.ShapeDtypeStruct((db, NSA_GROUPS, t, w_slc), F32)],
        compiler_params=_cparams(("parallel",)))(q, kc, vc, ov, bias)


def _sum_lane(g):
    return NSA_D * (1 - g)


def _slc_prompt_body(qi_ref, ki_ref, q_ref, ks_ref, vs_ref, sel_ref, place_ref, ep_ref, br_ref, o_ref,
                     qs_ref, m_ref, acc_ref, *, tk):
    step = pl.program_id(0)
    qi, ki = qi_ref[step], ki_ref[step]
    rows = NSA_HPG * TQ
    sub = tk // TQ

    @pl.when(ki == 0)
    def _():
        q = q_ref[...]
        for g in range(NSA_GROUPS):
            qs_ref[g] = _stack_heads(q, g).astype(BF16)
        m_ref[...] = jnp.full(m_ref.shape, NEG, F32)
        acc_ref[...] = jnp.zeros(acc_ref.shape, F32)

    near = (ki * sub + sub - 1) >= qi - 1

    def run(is_near):
        ks, vs = ks_ref[...], vs_ref[...]
        lane = lax.broadcasted_iota(jnp.int32, (tk, LANES), 1)
        for g in range(NSA_GROUPS):
            other = (lane >= NSA_D) if g == 0 else (lane < NSA_D)
            k_aug = jnp.where(other, ep_ref[g], ks)
            v_aug = jnp.where(lane == _sum_lane(g), jnp.ones((), BF16), vs)
            selpad = _dot(sel_ref[g], place_ref[g]).astype(BF16)
            q_aug = (qs_ref[g].reshape(NSA_HPG, TQ, LANES) + selpad[None]).reshape(rows, LANES)
            s = _dot_nt(q_aug, k_aug)
            if is_near:
                tok = qi * TQ - ki * tk + lax.broadcasted_iota(jnp.int32, (TQ, tk), 0)
                causal = jnp.where(lax.broadcasted_iota(jnp.int32, (TQ, tk), 1) <= tok, 0.0, NEG)
                s = (s.reshape(NSA_HPG, TQ, tk) + causal[None]).reshape(rows, tk)
                pieces = []
                for u in range(sub):
                    delta = qi - (ki * sub + u)
                    pieces.append(jnp.where(delta == 0, br_ref[g, 0], jnp.where(delta == 1, br_ref[g, 1], 0.0)))
                s = s + jnp.concatenate(pieces, axis=1)
            m_prev = m_ref[g]
            m_new = jnp.maximum(m_prev, jnp.max(s, axis=1, keepdims=True))
            p = jnp.exp2(s - _lane_tile(m_new, tk // LANES))
            acc_ref[g] = acc_ref[g] * jnp.exp2(m_prev - m_new) + _dot(p.astype(BF16), v_aug)
            m_ref[g] = m_new

    @pl.when(near)
    def _():
        run(True)

    @pl.when(jnp.logical_not(near))
    def _():
        run(False)

    @pl.when(ki == (qi * TQ + TQ - 1) // tk)
    def _():
        for g in range(NSA_GROUPS):
            acc = acc_ref[g]
            o = acc * (1.0 / acc[:, _sum_lane(g):_sum_lane(g) + 1])
            for m4, blk in enumerate(_unstack_heads(o, g, TQ)):
                col = (g * 4 + m4) * 128
                o_ref[:, col:col + 128] = blk


def _tile_bias(rel_bias):
    t = np.arange(TQ)[:, None]
    c = np.arange(TQ)[None, :]
    return jnp.stack([_bias_rows(rel_bias, d * TQ + t - c) for d in (0, 1)], axis=1)


def _slc_prompt(q, ksb, vsb, sel_neg, rel_bias, seq):
    tk = _key_tile(seq, 1024)
    rows = NSA_HPG * TQ
    nk = seq // tk
    bpt = tk // SLC_BLOCK
    qi, ki = _causal_schedule(seq // TQ, TQ, tk)
    b_rel = _tile_bias(rel_bias) - _far_bias(rel_bias, TQ)[:, None]
    sel_t = jnp.transpose(sel_neg[:, :, :nk * bpt].reshape(NSA_GROUPS, seq, nk, bpt), (0, 2, 1, 3))
    assert bpt <= NSA_D
    place = np.zeros((NSA_GROUPS, bpt, LANES), np.float32)
    onehot = np.zeros((NSA_GROUPS, tk, LANES), np.float32)
    for g in range(NSA_GROUPS):
        place[g, np.arange(bpt), _sum_lane(g) + np.arange(bpt)] = 1.0
        onehot[g, np.arange(tk), _sum_lane(g) + np.arange(tk) // SLC_BLOCK] = 1.0
    place, onehot = jnp.asarray(place).astype(BF16), jnp.asarray(onehot).astype(BF16)
    grid_spec = pltpu.PrefetchScalarGridSpec(
        num_scalar_prefetch=2, grid=(qi.shape[0],),
        in_specs=[pl.BlockSpec((TQ, 1024), lambda s, a, b: (a[s], 0)),
                  pl.BlockSpec((tk, 128), lambda s, a, b: (b[s], 0)),
                  pl.BlockSpec((tk, 128), lambda s, a, b: (b[s], 0)),
                  pl.BlockSpec((NSA_GROUPS, None, TQ, bpt), lambda s, a, b: (0, b[s], a[s], 0)),
                  pl.BlockSpec(place.shape, lambda s, a, b: (0, 0, 0)),
                  pl.BlockSpec(onehot.shape, lambda s, a, b: (0, 0, 0)),
                  pl.BlockSpec(b_rel.shape, lambda s, a, b: (0, 0, 0, 0))],
        out_specs=pl.BlockSpec((TQ, 1024), lambda s, a, b: (a[s], 0)),
        scratch_shapes=[pltpu.VMEM((NSA_GROUPS, rows, 128), BF16), pltpu.VMEM((NSA_GROUPS, rows, LANES), F32),
                        pltpu.VMEM((NSA_GROUPS, rows, 128), F32)])
    return pl.pallas_call(
        functools.partial(_slc_prompt_body, tk=tk), grid_spec=grid_spec, name="slc_prompt",
        out_shape=jax.ShapeDtypeStruct((seq, 1024), F32),
        compiler_params=_cparams(("arbitrary",)))(qi, ki, q, ksb, vsb, sel_t, place, onehot, b_rel)


N_WIN_BLK = WINDOW // TQ + 1


def _win_prompt_body(q_ref, *refs):
    k_refs, v_refs = refs[:N_WIN_BLK], refs[N_WIN_BLK:2 * N_WIN_BLK]
    bn_ref, bf_ref, o_ref = refs[2 * N_WIN_BLK:]
    i = pl.program_id(0)
    q = q_ref[...]
    rows = NSA_HPG * TQ
    n_keys = N_WIN_BLK * TQ
    k = jnp.concatenate([r[...] for r in k_refs], axis=0)
    v = jnp.concatenate([r[...] for r in v_refs], axis=0)
    t = lax.broadcasted_iota(jnp.int32, (rows, n_keys), 0) & (TQ - 1)
    c = lax.broadcasted_iota(jnp.int32, (rows, n_keys), 1)
    dist = WINDOW + t - c
    mask = (dist >= 0) & (dist < WINDOW) & ((i - (N_WIN_BLK - 1)) * TQ + c >= 0)
    for g in range(NSA_GROUPS):
        qg = _stack_heads(q, g).astype(BF16)
        far = jnp.broadcast_to(bf_ref[g], (rows, n_keys - 2 * TQ))
        bias = jnp.concatenate([far, bn_ref[g, 1], bn_ref[g, 0]], axis=1)
        s = jnp.where(mask, _dot_nt(qg, k) + bias, NEG)
        m = jnp.max(s, axis=1, keepdims=True)
        p = jnp.exp2(s - m)
        o = _dot(p.astype(BF16), v) / jnp.sum(p, axis=1, keepdims=True)
        for m4, blk in enumerate(_unstack_heads(o, g, TQ)):
            col = (g * 4 + m4) * 128
            o_ref[:, col:col + 128] = blk


def _win_prompt(q, kwb, vwb, rel_bias, seq):
    b_near = _tile_bias(rel_bias)
    b_far = _far_bias(rel_bias, TQ)

    def kv_spec(u):
        return pl.BlockSpec((TQ, 128), lambda i, _u=u: (jnp.maximum(i - (N_WIN_BLK - 1) + _u, 0), 0))

    return pl.pallas_call(
        _win_prompt_body, grid=(seq // TQ,), name="win_prompt",
        in_specs=[pl.BlockSpec((TQ, 1024), lambda i: (i, 0))] + [kv_spec(u) for u in range(N_WIN_BLK)] * 2
        + [_const_spec(b_near.shape), _const_spec(b_far.shape)],
        out_specs=pl.BlockSpec((TQ, 1024), lambda i: (i, 0)),
        out_shape=jax.ShapeDtypeStruct((seq, 1024), F32),
        compiler_params=_cparams(("parallel",)))(q, *([kwb] * N_WIN_BLK), *([vwb] * N_WIN_BLK), b_near, b_far)


def _slc_sample_body(pt_ref, q_ref, sel_ref, seln_ref, ex_ref, bias_ref, bnew_ref, kn_ref, vn_ref, k_hbm, v_hbm,
                     o_ref, kbuf, vbuf, sems, qs_ref, m_ref, l_ref, acc_ref, *, pp, t, li):
    slot = _paged_pipeline(pt_ref, (k_hbm, v_hbm), (kbuf, vbuf), (sems.at[0], sems.at[1]), li, pp)
    pc = pl.program_id(1)
    rows = NSA_HPG * t
    tk = pp * PAGE

    @pl.when(pc == 0)
    def _():
        q = q_ref[...]
        for g in range(NSA_GROUPS):
            qs_ref[g] = _stack_heads(q, g).astype(BF16)
            _flash_init(m_ref.at[g], l_ref.at[g], acc_ref.at[g])

    feat = NSA_GROUPS * NSA_D
    ks = [kbuf[slot, j].reshape(feat, PAGE).astype(BF16) for j in range(pp)]
    vs = [vbuf[slot, j].reshape(feat, PAGE).astype(BF16) for j in range(pp)]
    ex = ex_ref[...]
    for g in range(NSA_GROUPS):
        qg = qs_ref[g]
        s = jnp.concatenate([_dot(qg, ks[j]) for j in range(pp)], axis=1) + bias_ref[g]
        selx = _dot(sel_ref[g].astype(BF16), ex)
        mask = jnp.concatenate([selx] * NSA_HPG, axis=0) > 0.5

        def pv(p):
            acc = _dot_nt(p[:, :PAGE], vs[0])
            for j in range(1, pp):
                acc = acc + _dot_nt(p[:, j * PAGE:(j + 1) * PAGE], vs[j])
            return acc

        _online_update(s, mask, pv, m_ref.at[g], l_ref.at[g], acc_ref.at[g])

    @pl.when(pc == pl.num_programs(1) - 1)
    def _():
        kn, vn = _pad_rows16(kn_ref[...]).astype(BF16), _pad_rows16(vn_ref[...]).astype(BF16)
        tok = lax.broadcasted_iota(jnp.int32, (rows, NEW_PAD), 0) & (t - 1)
        key = lax.broadcasted_iota(jnp.int32, (rows, NEW_PAD), 1)
        for g in range(NSA_GROUPS):
            s = _dot_nt(qs_ref[g], kn) + bnew_ref[g]
            seln = jnp.concatenate([seln_ref[g]] * NSA_HPG, axis=0) > 0.5
            _online_update(s, seln & (key <= tok), lambda p: _dot(p, vn), m_ref.at[g], l_ref.at[g], acc_ref.at[g])
            o = _flash_result(l_ref.at[g], acc_ref.at[g])
            for m4, blk in enumerate(_unstack_heads(o, g, t)):
                col = (g * 4 + m4) * 128
                o_ref[:, col:col + 128] = blk


def _slc_sample(q, sel, ks_new, vs_new, pool_k, pool_v, li, page_table, rel_bias, t):
    db, n_pages = page_table.shape
    past = n_pages * PAGE
    pp = _pages_per_step(n_pages)
    n_steps = n_pages // pp
    tk = pp * PAGE
    bpt = tk // SLC_BLOCK
    rows = NSA_HPG * t
    sel_past = sel[..., :n_steps * bpt].reshape(db, NSA_GROUPS, t, n_steps, bpt)
    sel_past = jnp.transpose(sel_past, (0, 3, 1, 2, 4))
    new_blk = past // SLC_BLOCK
    sel_new = jnp.broadcast_to(sel[..., new_blk:new_blk + 1], (db, NSA_GROUPS, t, NEW_PAD))
    expand = jnp.asarray((np.arange(bpt)[:, None] == np.arange(tk)[None, :] // SLC_BLOCK).astype(np.float32)).astype(BF16)
    tt = np.arange(t)[:, None]
    assert tk >= FAR_DIST
    last_keys = (n_steps - 1) * tk + np.arange(tk)[None, :]
    bias = jnp.stack([jnp.broadcast_to(_far_bias(rel_bias, t), (NSA_GROUPS, rows, tk)),
                      _bias_rows(rel_bias, past + tt - last_keys)])
    b_new = _bias_rows(rel_bias, tt - np.arange(NEW_PAD)[None, :])
    hbm_spec = pl.BlockSpec(memory_space=pl.ANY)
    page_buf = pltpu.VMEM((2, pp, NSA_GROUPS, NSA_D, PAGE), F32)
    grid_spec = pltpu.PrefetchScalarGridSpec(
        num_scalar_prefetch=1, grid=(db, n_steps),
        in_specs=[pl.BlockSpec((t, 1024), lambda b, c, pt: (b, 0)),
                  pl.BlockSpec((None, None, NSA_GROUPS, t, bpt), lambda b, c, pt: (b, c, 0, 0, 0)),
                  pl.BlockSpec((None, NSA_GROUPS, t, NEW_PAD), lambda b, c, pt: (b, 0, 0, 0)),
                  pl.BlockSpec(expand.shape, lambda b, c, pt: (0, 0)),
                  pl.BlockSpec((None, NSA_GROUPS, rows, tk),
                               lambda b, c, pt: (jnp.where(c == n_steps - 1, 1, 0), 0, 0, 0)),
                  pl.BlockSpec(b_new.shape, lambda b, c, pt: (0, 0, 0)),
                  pl.BlockSpec((t, 128), lambda b, c, pt: (b, 0)),
                  pl.BlockSpec((t, 128), lambda b, c, pt: (b, 0)), hbm_spec, hbm_spec],
        out_specs=pl.BlockSpec((t, 1024), lambda b, c, pt: (b, 0)),
        scratch_shapes=[page_buf, page_buf, pltpu.SemaphoreType.DMA((2, 2)),
                        pltpu.VMEM((NSA_GROUPS, rows, 128), BF16), pltpu.VMEM((NSA_GROUPS, rows, 1), F32),
                        pltpu.VMEM((NSA_GROUPS, rows, 1), F32), pltpu.VMEM((NSA_GROUPS, rows, 128), F32)])
    return pl.pallas_call(
        functools.partial(_slc_sample_body, pp=pp, t=t, li=li), grid_spec=grid_spec, name="slc_sample",
        out_shape=jax.ShapeDtypeStruct((db * t, 1024), F32),
        compiler_params=_cparams(("arbitrary", "arbitrary")))(
            page_table, q, sel_past, sel_new, expand, bias, b_new, ks_new, vs_new,
            _feature_major(pool_k), _feature_major(pool_v))


def _win_sample_body(q_ref, wk_ref, wv_ref, kn_ref, vn_ref, bias_ref, o_ref, *, t, w_buf):
    q = q_ref[...]
    rows = NSA_HPG * t
    feat = NSA_GROUPS * NSA_D
    kt = wk_ref[...].reshape(feat, w_buf).astype(BF16)
    vt = wv_ref[...].reshape(feat, w_buf).astype(BF16)
    kn, vn = _pad_rows16(kn_ref[...]).astype(BF16), _pad_rows16(vn_ref[...]).astype(BF16)
    n_keys = w_buf + NEW_PAD
    tok = lax.broadcasted_iota(jnp.int32, (rows, n_keys), 0) & (t - 1)
    c = lax.broadcasted_iota(jnp.int32, (rows, n_keys), 1)
    dist = w_buf + tok - c
    mask = (dist >= 0) & (dist < WINDOW)
    for g in range(NSA_GROUPS):
        qg = _stack_heads(q, g).astype(BF16)
        s = jnp.concatenate([_dot(qg, kt), _dot_nt(qg, kn)], axis=1) + bias_ref[g]
        s = jnp.where(mask, s, NEG)
        m = jnp.max(s, axis=1, keepdims=True)
        p = jnp.exp2(s - m)
        pb = p.astype(BF16)
        o = (_dot_nt(pb[:, :w_buf], vt) + _dot(pb[:, w_buf:], vn)) / jnp.sum(p, axis=1, keepdims=True)
        for m4, blk in enumerate(_unstack_heads(o, g, t)):
            col = (g * 4 + m4) * 128
            o_ref[:, col:col + 128] = blk


def _win_sample(q, wk_buf, wv_buf, kw_new, vw_new, rel_bias, t):
    db, w_buf = wk_buf.shape[0], wk_buf.shape[1]
    tt = np.arange(t)[:, None]
    bias = _bias_rows(rel_bias, w_buf + tt - np.arange(w_buf + NEW_PAD)[None, :])
    wk = jnp.transpose(wk_buf, (0, 2, 3, 1))
    wv = jnp.transpose(wv_buf, (0, 2, 3, 1))
    buf_spec = pl.BlockSpec((None, NSA_GROUPS, NSA_D, w_buf), lambda b: (b, 0, 0, 0))
    return pl.pallas_call(
        functools.partial(_win_sample_body, t=t, w_buf=w_buf), grid=(db,), name="win_sample",
        in_specs=[pl.BlockSpec((t, 1024), lambda b: (b, 0)), buf_spec, buf_spec, pl.BlockSpec((t, 128), lambda b: (b, 0)),
                  pl.BlockSpec((t, 128), lambda b: (b, 0)), _const_spec(bias.shape)],
        out_specs=pl.BlockSpec((t, 1024), lambda b: (b, 0)),
        out_shape=jax.ShapeDtypeStruct((db * t, 1024), F32),
        compiler_params=_cparams(("parallel",)))(q, wk, wv, kw_new, vw_new, bias)


def kernel(x_prompt, x_sample, cache_mla_latent, cache_mla_krope, state_ssm, state_conv, cache_cmp_k, cache_cmp_v, cache_slc_k, cache_slc_v, state_win_k, state_win_v, page_table, rel_bias, w_in_a, q_norm, w_uq, kv_norm, w_uk, w_uv, conv_w, conv_b, dt_bias, a_log, d_skip, ssm_norm, w_out_a, ln_a_g, ln_a_b, w_in_c, cmp_pe_k, cmp_w1_k, cmp_w2_k, cmp_pe_v, cmp_w1_v, cmp_w2_v, w_out_c, ln_c_g, ln_c_b):
    bp, sp = x_prompt.shape[:2]
    db, ss = x_sample.shape[:2]
    n_pages = page_table.shape[1]
    past = n_pages * PAGE
    assert bp == 1 and x_prompt.shape[2] == D_MODEL and cache_mla_latent.shape[2] == PAGE
    assert sp % TQ == 0 and sp >= WINDOW and ss == 8
    assert past % SLC_BLOCK == 0 and ss < CMP_STRIDE and past % CMP_STRIDE == 0 and past >= WINDOW
    assert state_win_k.shape[2] == WINDOW

    hp = x_prompt.reshape(sp, D_MODEL)
    hs = x_sample.reshape(db * ss, D_MODEL)
    pos_p = jnp.arange(sp)
    pos_s = jnp.tile(past + jnp.arange(ss), db)
    st = {}

    pa = _prep_a(w_in_a[0], q_norm[0], w_uq[0], kv_norm[0], w_uk[0], w_uv[0], dt_bias[0], w_out_a[0])
    qlat, qrope, lat, latb, kr, krb, g, z, xbc, dt = _a_front(hp, pos_p, pa)
    o_lat = _mla_prompt(qlat.reshape(sp * MLA_HEADS, KV_RANK), qrope.reshape(sp * MLA_HEADS, ROPE), latb, krb)
    y, cbuf, hst = _ssd(xbc, dt, jnp.zeros((1, CONV_W - 1, CONV_DIM), F32),
                        jnp.zeros((1, SSM_HEADS, SSM_HEADDIM, D_STATE), F32), conv_w[0], conv_b[0], a_log[0],
                        d_skip[0], 1)
    hp = _a_back(hp, o_lat.reshape(sp, MLA_HEADS * KV_RANK), g, y, z, pa, ssm_norm[0], ln_a_g[0], ln_a_b[0])
    st['p_lat'], st['p_krope'] = lat.reshape(1, 1, sp, KV_RANK), kr.reshape(1, 1, sp, ROPE)
    st['p_ssm'], st['p_conv'] = hst[None], cbuf[None]
    qlat, qrope, lat, latb, kr, krb, g, z, xbc, dt = _a_front(hs, pos_s, pa)
    o_lat = _mla_sample(qlat.reshape(db * ss * MLA_HEADS, KV_RANK), qrope.reshape(db * ss * MLA_HEADS, ROPE), lat, kr,
                        cache_mla_latent, cache_mla_krope, 0, page_table, ss)
    y, cbuf, hst = _ssd(xbc, dt, state_conv[0], state_ssm[0], conv_w[0], conv_b[0], a_log[0], d_skip[0], db)
    hs = _a_back(hs, o_lat.reshape(db * ss, MLA_HEADS * KV_RANK), g, y, z, pa, ssm_norm[0], ln_a_g[0], ln_a_b[0])
    st['s_lat'], st['s_krope'] = lat.reshape(1, db, ss, KV_RANK), kr.reshape(1, db, ss, ROPE)
    st['s_ssm'], st['s_conv'] = hst[None], cbuf[None]

    pc = _prep_c(w_in_c[0], w_out_c[0])
    pk = _prep_cmp(cmp_pe_k[0], cmp_w1_k[0], cmp_w2_k[0])
    pv = _prep_cmp(cmp_pe_v[0], cmp_w1_v[0], cmp_w2_v[0])
    kv5 = lambda a, b_, s_: a.reshape(1, b_, s_, NSA_GROUPS, NSA_D)
    q, kc_r, vc_r, ks, vs, kw, vw, ksb, vsb, kwb, vwb, og = _c_front(hp, pc)
    kc, vc = _cmp_finish(_cmp_u_prompt(kc_r, pk[0]), _cmp_u_prompt(vc_r, pv[0]), pk, pv, 1)
    o_c, sel = _cmp_prompt(q, kc[0], vc[0], rel_bias, sp)
    o_s = _slc_prompt(q, ksb, vsb, sel, rel_bias, sp)
    o_w = _win_prompt(q, kwb, vwb, rel_bias, sp)
    hp = _c_back(hp, o_c, o_s, o_w, og, pc, ln_c_g[0], ln_c_b[0])
    keep = min(WINDOW, sp)
    for n, a in (('p_cmp_k', kc_r), ('p_cmp_v', vc_r), ('p_slc_k', ks), ('p_slc_v', vs)):
        st[n] = kv5(a, 1, sp)
    st['p_win_k'], st['p_win_v'] = kv5(kw[sp - keep:], 1, keep), kv5(vw[sp - keep:], 1, keep)
    q, kc_r, vc_r, ks, vs, kw, vw, ksb, vsb, kwb, vwb, og = _c_front(hs, pc)
    uk, uv = _cmp_u_paged(cache_cmp_k, cache_cmp_v, 0, page_table, pk[0], pv[0])
    n_chunks = uk.shape[1]
    kc, vc = _cmp_finish(uk.reshape(db * n_chunks, -1), uv.reshape(db * n_chunks, -1), pk, pv, db)
    o_c, sel = _cmp_sample(q, kc, vc, rel_bias, past, ss)
    o_s = _slc_sample(q, sel, ks, vs, cache_slc_k, cache_slc_v, 0, page_table, rel_bias, ss)
    o_w = _win_sample(q, state_win_k[0], state_win_v[0], kw, vw, rel_bias, ss)
    hs = _c_back(hs, o_c, o_s, o_w, og, pc, ln_c_g[0], ln_c_b[0])
    for n, a in (('s_cmp_k', kc_r), ('s_cmp_v', vc_r), ('s_slc_k', ks), ('s_slc_v', vs)):
        st[n] = kv5(a, db, ss)
    new5 = lambda a: a.reshape(db, ss, NSA_GROUPS, NSA_D)
    st['s_win_k'] = jnp.concatenate([state_win_k[0], new5(kw)], 1)[None, :, ss:]
    st['s_win_v'] = jnp.concatenate([state_win_v[0], new5(vw)], 1)[None, :, ss:]

    names = ['p_lat', 'p_krope', 'p_ssm', 'p_conv', 'p_cmp_k', 'p_cmp_v', 'p_slc_k', 'p_slc_v', 'p_win_k', 'p_win_v',
             's_lat', 's_krope', 's_ssm', 's_conv', 's_cmp_k', 's_cmp_v', 's_slc_k', 's_slc_v', 's_win_k', 's_win_v']
    return (hp.reshape(bp, sp, D_MODEL), hs.reshape(db, ss, D_MODEL)) + tuple(st[n] for n in names)
```

```python
import functools
import math

import numpy as np
import jax
import jax.numpy as jnp
from jax import lax
from jax.experimental import pallas as pl
from jax.experimental.pallas import tpu as pltpu

F32 = jnp.float32
BF16 = jnp.bfloat16

D_MODEL = 1024
DEPTH = 2
DEEPNORM_ALPHA = (2.0 * DEPTH) ** 0.25
LN_EPS = 1e-5
RMS_EPS = 1e-6
NEG = -1e30
PAGE = 128

MLA_HEADS = 8
Q_RANK = 384
KV_RANK = 256
NOPE = 64
ROPE = 32
MLA_V = 64
MLA_WIDTH = MLA_HEADS * MLA_V
LOG2E = math.log2(math.e)
MLA_SCALE = (NOPE + ROPE) ** -0.5 * LOG2E
ROPE_BASE = 10000.0

SSM_HEADDIM = 64
D_INNER = 1024
SSM_HEADS = 16
SSM_GROUPS = 2
D_STATE = 128
CONV_W = 4
CONV_DIM = D_INNER + 2 * SSM_GROUPS * D_STATE
SSD_CHUNK = 128

NSA_HEADS = 16
NSA_GROUPS = 2
NSA_HPG = 8
NSA_D = 64
NSA_SCALE = NSA_D ** -0.5 * LOG2E
CMP_LEN = 32
CMP_STRIDE = 16
SLC_BLOCK = 64
N_SEL = 16
WINDOW = 512
NUM_BUCKETS = 32
MAX_DISTANCE = 128
FAR_DIST = 128

TQ = 128
VMEM_LIMIT = 56 * 1024 * 1024


def _cparams(sem):
    return pltpu.CompilerParams(dimension_semantics=sem, vmem_limit_bytes=VMEM_LIMIT)


def _const_spec(shape):
    nd = len(shape)
    return pl.BlockSpec(shape, lambda *a, _nd=nd: (0,) * _nd)


def _dot(a, b):
    return jnp.dot(a, b, preferred_element_type=F32)


def _dot_nt(a, b):
    return lax.dot_general(a, b, (((1,), (1,)), ((), ())), preferred_element_type=F32)


def _dot_tn(a, b):
    return lax.dot_general(a, b, (((0,), (0,)), ((), ())), preferred_element_type=F32)


def _sigmoid(x):
    return 1.0 / (1.0 + jnp.exp(-x))


def _silu(x):
    return x * _sigmoid(x)


def _rms(x, g):
    return x * lax.rsqrt(jnp.mean(x * x, axis=-1, keepdims=True) + RMS_EPS) * g


def _deepnorm_ln(x, out, g, b):
    h = DEEPNORM_ALPHA * x + out
    mu = jnp.mean(h, axis=-1, keepdims=True)
    d = h - mu
    var = jnp.mean(d * d, axis=-1, keepdims=True)
    return d * lax.rsqrt(var + LN_EPS) * g + b


def _t5_bucket_np(dist):
    n = np.maximum(dist, 0)
    exact = NUM_BUCKETS // 2
    nf = np.maximum(n, exact).astype(np.float32)
    large = exact + (np.log(nf / np.float32(exact)) / np.float32(math.log(MAX_DISTANCE / exact))
                     * np.float32(NUM_BUCKETS - exact)).astype(np.int32)
    return np.where(n < exact, n, np.minimum(large, NUM_BUCKETS - 1)).astype(np.int32)


def _row_tile(rows):
    return 256 if rows % 256 == 0 else rows


def _tok_call(body, rows, row_ins, const_ins, outs):
    tm = _row_tile(rows)
    in_specs = [pl.BlockSpec((tm, a.shape[1]), lambda i: (i, 0)) for a in row_ins]
    in_specs += [_const_spec(a.shape) for a in const_ins]
    out_specs = [pl.BlockSpec((tm, w), lambda i: (i, 0)) for w, _ in outs]
    out_shape = [jax.ShapeDtypeStruct((rows, w), dt) for w, dt in outs]
    return pl.pallas_call(
        body, grid=(rows // tm,), in_specs=in_specs, out_specs=out_specs, out_shape=out_shape,
        name=body.__name__.strip("_").removesuffix("_body"), compiler_params=_cparams(("parallel",)))(*row_ins, *const_ins)


A_CQ, A_CKV, A_G, A_Z, A_XBC, A_MISC, A_END = 0, 384, 640, 1152, 2176, 3712, 3840


def _a_front_body(x_ref, ccq_ref, ssq_ref, cck_ref, ssk_ref, w_ref, qn_ref, kvn_ref, wuq_ref, wuk_ref, dtb_ref,
                  qlat_ref, qrope_ref, lat_ref, latb_ref, kr_ref, krb_ref, g_ref, z_ref, xbc_ref, dt_ref):
    xb = x_ref[...].astype(BF16)

    def proj(lo, hi):
        return _dot(xb, w_ref[:, lo:hi])

    cqn = _rms(proj(A_CQ, A_CKV), qn_ref[...])
    qall = _dot(cqn.astype(BF16), wuq_ref[...])
    qrope = (qall[:, 512:768] * ccq_ref[...] + qall[:, 768:1024] * ssq_ref[...]) * MLA_SCALE
    qrope_ref[...] = qrope.astype(BF16)
    qlat = _dot(qall[:, :512].astype(BF16), wuk_ref[...]) * MLA_SCALE
    qlat_ref[...] = qlat.astype(BF16)
    lat = _rms(proj(A_CKV, A_G), kvn_ref[...])
    lat_ref[...] = lat
    latb_ref[...] = lat.astype(BF16)
    g_ref[...] = proj(A_G, A_Z)
    z_ref[...] = proj(A_Z, A_XBC)
    xbc_ref[...] = proj(A_XBC, A_MISC)
    misc = proj(A_MISC, A_END)
    kr = misc[:, 0:32] * cck_ref[...] + misc[:, 32:64] * ssk_ref[...]
    kr_ref[...] = kr
    krb_ref[...] = kr.astype(BF16)
    v = misc[:, 64:80] + dtb_ref[...]
    dt_ref[...] = jnp.maximum(v, 0.0) + jnp.log1p(jnp.exp(-jnp.abs(v)))


def _rot_cols(w):
    h = ROPE // 2
    return jnp.concatenate([-w[..., h:], w[..., :h]], axis=-1)


def _prep_a(w_in, q_norm, w_uq, kv_norm, w_uk, w_uv, dt_bias, w_out):
    cq, ckv, kr, g, z, xbc, dtw = jnp.split(w_in, np.cumsum([Q_RANK, KV_RANK, ROPE, MLA_WIDTH, D_INNER, CONV_DIM])[:].tolist(),
                                            axis=1)
    pad = jnp.zeros((w_in.shape[0], A_END - A_MISC - 2 * ROPE - SSM_HEADS), w_in.dtype)
    w1 = jnp.concatenate([cq, ckv, g, z, xbc, kr, _rot_cols(kr), dtw, pad], axis=1).astype(BF16)
    wq = w_uq.reshape(Q_RANK, MLA_HEADS, NOPE + ROPE)
    wq_n = wq[:, :, :NOPE].reshape(Q_RANK, MLA_HEADS * NOPE)
    wq_r = wq[:, :, NOPE:]
    wuq = jnp.concatenate([wq_n, wq_r.reshape(Q_RANK, -1), _rot_cols(wq_r).reshape(Q_RANK, -1)], axis=1).astype(BF16)
    eye = jnp.eye(MLA_HEADS, dtype=w_uk.dtype)
    wuk = (jnp.transpose(w_uk, (1, 2, 0))[:, :, None, :] * eye[:, None, :, None]).reshape(
        MLA_HEADS * NOPE, MLA_HEADS * KV_RANK).astype(BF16)
    wuv = (jnp.transpose(w_uv, (1, 0, 2))[:, :, None, :] * eye[:, None, :, None]).reshape(
        MLA_HEADS * KV_RANK, MLA_WIDTH).astype(BF16)
    return dict(w1=w1, qn=q_norm.reshape(1, -1), kvn=kv_norm.reshape(1, -1), wuq=wuq, wuk=wuk, wuv=wuv,
                dtb=dt_bias.reshape(1, -1), wout=w_out.astype(BF16))


def _rope_tables(pos):
    half = ROPE // 2
    inv = ROPE_BASE ** (-jnp.arange(half, dtype=F32) / half)
    ang = pos.astype(F32)[:, None] * inv[None, :]
    c, s = jnp.cos(ang), jnp.sin(ang)
    cck, ssk = jnp.concatenate([c, c], 1), jnp.concatenate([s, s], 1)
    return jnp.tile(cck, (1, MLA_HEADS)), jnp.tile(ssk, (1, MLA_HEADS)), cck, ssk


def _a_front(x, pos, pa):
    rows = x.shape[0]
    ccq, ssq, cck, ssk = _rope_tables(pos)
    outs = [(MLA_HEADS * KV_RANK, BF16), (MLA_HEADS * ROPE, BF16), (KV_RANK, F32), (KV_RANK, BF16), (ROPE, F32),
            (ROPE, BF16), (MLA_WIDTH, F32), (D_INNER, F32), (CONV_DIM, F32), (SSM_HEADS, F32)]
    return _tok_call(_a_front_body, rows, [x, ccq, ssq, cck, ssk],
                     [pa["w1"], pa["qn"], pa["kvn"], pa["wuq"], pa["wuk"], pa["dtb"]], outs)


def _a_back_body(x_ref, ol_ref, g_ref, y_ref, z_ref, wuv_ref, sn_ref, wo_ref, lg_ref, lb_ref, o_ref):
    o_mla = _dot(ol_ref[...], wuv_ref[...])
    a = o_mla * _silu(g_ref[...])
    yn = _rms(y_ref[...] * _silu(z_ref[...]), sn_ref[...])
    out = _dot(a.astype(BF16), wo_ref[:MLA_WIDTH, :]) + _dot(yn.astype(BF16), wo_ref[MLA_WIDTH:, :])
    o_ref[...] = _deepnorm_ln(x_ref[...], out, lg_ref[...], lb_ref[...])


def _a_back(x, o_lat, g, y, z, pa, ssm_norm, ln_g, ln_b):
    (out,) = _tok_call(_a_back_body, x.shape[0], [x, o_lat, g, y, z],
                       [pa["wuv"], ssm_norm.reshape(1, -1), pa["wout"], ln_g.reshape(1, -1), ln_b.reshape(1, -1)],
                       [(D_MODEL, F32)])
    return out


C_Q, C_KV, C_OG, C_END = 0, 1024, 1792, 2816


def _c_front_body(x_ref, w_ref, q_ref, kc_ref, vc_ref, ks_ref, vs_ref, kw_ref, vw_ref, ksb_ref, vsb_ref, kwb_ref,
                  vwb_ref, og_ref):
    xb = x_ref[...].astype(BF16)
    q_ref[...] = _dot(xb, w_ref[:, C_Q:C_KV]) * NSA_SCALE
    kv = _dot(xb, w_ref[:, C_KV:C_OG])
    for j, r in enumerate((kc_ref, vc_ref, ks_ref, vs_ref, kw_ref, vw_ref)):
        r[...] = kv[:, 128 * j:128 * (j + 1)]
    for j, r in enumerate((ksb_ref, vsb_ref, kwb_ref, vwb_ref)):
        r[...] = kv[:, 128 * (j + 2):128 * (j + 3)].astype(BF16)
    og_ref[...] = _dot(xb, w_ref[:, C_OG:C_END])


def _prep_c(w_in, w_out):
    n_kv = 6 * NSA_GROUPS * NSA_D
    w_main = jnp.concatenate([w_in[:, :1024 + n_kv], w_in[:, 1024 + n_kv + 3 * NSA_HEADS:]], axis=1).astype(BF16)
    w_g = w_in[:, 1024 + n_kv:1024 + n_kv + 3 * NSA_HEADS]
    w_grep = jnp.repeat(w_g, NSA_D, axis=1).astype(BF16)
    return dict(w_main=w_main, w_grep=w_grep, wout=w_out.astype(BF16))


def _c_front(x, pc):
    outs = [(1024, F32)] + [(128, F32)] * 6 + [(128, BF16)] * 4 + [(1024, F32)]
    return _tok_call(_c_front_body, x.shape[0], [x], [pc["w_main"]], outs)


def _c_back_body(x_ref, oc_ref, os_ref, ow_ref, og_ref, wg_ref, wo_ref, lg_ref, lb_ref, o_ref):
    x = x_ref[...]
    gates = _sigmoid(_dot(x.astype(BF16), wg_ref[...]))
    o = gates[:, :1024] * oc_ref[...] + gates[:, 1024:2048] * os_ref[...] + gates[:, 2048:] * ow_ref[...]
    out = _dot((o * _silu(og_ref[...])).astype(BF16), wo_ref[...])
    o_ref[...] = _deepnorm_ln(x, out, lg_ref[...], lb_ref[...])


def _c_back(x, o_c, o_s, o_w, og, pc, ln_g, ln_b):
    (out,) = _tok_call(_c_back_body, x.shape[0], [x, o_c, o_s, o_w, og],
                       [pc["w_grep"], pc["wout"], ln_g.reshape(1, -1), ln_b.reshape(1, -1)], [(D_MODEL, F32)])
    return out


def _online_update(s, mask, v_fn, m_ref, l_ref, acc_ref):
    if mask is not None:
        s = jnp.where(mask, s, NEG)
    m_prev = m_ref[...]
    m_new = jnp.maximum(m_prev, jnp.max(s, axis=1, keepdims=True))
    p = jnp.exp2(s - m_new)
    if mask is not None:
        p = jnp.where(mask, p, 0.0)
    corr = jnp.exp2(m_prev - m_new)
    l_ref[...] = l_ref[...] * corr + jnp.sum(p, axis=1, keepdims=True)
    acc_ref[...] = acc_ref[...] * corr + v_fn(p.astype(BF16))
    m_ref[...] = m_new


LANES = 128


def _lane_tile(x, n):
    return x if n == 1 else jnp.concatenate([x] * n, axis=1)


def _flash_step(s, v_fn, m_ref, l_ref, acc_ref):
    m_prev = m_ref[...]
    m_new = jnp.maximum(m_prev, jnp.max(s, axis=1, keepdims=True))
    p = jnp.exp2(s - _lane_tile(m_new, s.shape[1] // LANES))
    corr = jnp.exp2(m_prev - m_new)
    l_ref[...] = l_ref[...] * corr + jnp.sum(p, axis=1, keepdims=True)
    dv = acc_ref.shape[-1]
    acc_ref[...] = acc_ref[...] * _lane_tile(corr, dv // LANES) + v_fn(p.astype(BF16))
    m_ref[...] = m_new


def _flash_out(l_ref, acc_ref):
    dv = acc_ref.shape[-1]
    inv = 1.0 / l_ref[...]
    return acc_ref[...] * _lane_tile(inv, dv // LANES)


NEW_PAD = 16


def _pad_rows16(x):
    return jnp.concatenate([x, jnp.zeros((NEW_PAD - x.shape[0], x.shape[1]), x.dtype)], axis=0)


def _flash_init(m_ref, l_ref, acc_ref):
    m_ref[...] = jnp.full(m_ref.shape, NEG, F32)
    l_ref[...] = jnp.zeros(l_ref.shape, F32)
    acc_ref[...] = jnp.zeros(acc_ref.shape, F32)


def _flash_result(l_ref, acc_ref):
    l = l_ref[...]
    return acc_ref[...] / jnp.where(l > 0.0, l, 1.0)


def _causal_schedule(n_q, tq, tk):
    qi, ki = [], []
    for i in range(n_q):
        last = (i * tq + tq - 1) // tk
        for k in range(last + 1):
            qi.append(i)
            ki.append(k)
    return jnp.asarray(np.array(qi, np.int32)), jnp.asarray(np.array(ki, np.int32))


def _key_tile(seq, want):
    tk = want
    while seq % tk:
        tk //= 2
    return max(tk, TQ)


def _mla_prompt_body(qi_ref, ki_ref, ql_ref, qr_ref, lat_ref, kr_ref, o_ref, m_ref, l_ref, acc_ref, *, tk):
    step = pl.program_id(0)
    qi, ki = qi_ref[step], ki_ref[step]

    last = (qi * TQ + TQ - 1) // tk

    @pl.when(ki == 0)
    def _():
        _flash_init(m_ref, l_ref, acc_ref)

    def run(diagonal):
        ts = tk // 2 if tk >= 2 * LANES else tk
        for h in range(tk // ts):
            lat = lat_ref[h * ts:(h + 1) * ts, :]
            s = _dot_nt(ql_ref[...], lat) + _dot_nt(qr_ref[...], kr_ref[h * ts:(h + 1) * ts, :])
            if diagonal:
                rows = s.shape[0]
                tok = qi * TQ - ki * tk - h * ts + (lax.broadcasted_iota(jnp.int32, (rows, ts), 0) >> 3)
                s = jnp.where(lax.broadcasted_iota(jnp.int32, (rows, ts), 1) <= tok, s, NEG)
            _flash_step(s, lambda p, lat=lat: _dot(p, lat), m_ref, l_ref, acc_ref)

    @pl.when(ki < last)
    def _():
        run(False)

    @pl.when(ki == last)
    def _():
        run(True)
        o_ref[...] = _flash_out(l_ref, acc_ref).astype(o_ref.dtype)


def _mla_prompt(ql, qr, latb, krb):
    seq = latb.shape[0]
    tk = _key_tile(seq, 1024)
    rows = TQ * MLA_HEADS
    qi, ki = _causal_schedule(seq // TQ, TQ, tk)
    grid_spec = pltpu.PrefetchScalarGridSpec(
        num_scalar_prefetch=2, grid=(qi.shape[0],),
        in_specs=[pl.BlockSpec((rows, KV_RANK), lambda s, q, k: (q[s], 0)),
                  pl.BlockSpec((rows, ROPE), lambda s, q, k: (q[s], 0)),
                  pl.BlockSpec((tk, KV_RANK), lambda s, q, k: (k[s], 0)),
                  pl.BlockSpec((tk, ROPE), lambda s, q, k: (k[s], 0))],
        out_specs=pl.BlockSpec((rows, KV_RANK), lambda s, q, k: (q[s], 0)),
        scratch_shapes=[pltpu.VMEM((rows, LANES), F32), pltpu.VMEM((rows, LANES), F32),
                        pltpu.VMEM((rows, KV_RANK), F32)])
    return pl.pallas_call(
        functools.partial(_mla_prompt_body, tk=tk), grid_spec=grid_spec, name="mla_prompt",
        out_shape=jax.ShapeDtypeStruct((seq * MLA_HEADS, KV_RANK), BF16),
        compiler_params=_cparams(("arbitrary",)))(qi, ki, ql, qr, latb, krb)


def _pages_per_step(n_pages):
    for pp in (32, 16, 8, 4, 2):
        if n_pages % pp == 0:
            return pp
    return 1


def _page_copy(pool_ref, buf_ref, sem_ref, li, page, slot, j):
    return pltpu.make_async_copy(pool_ref.at[li, page], buf_ref.at[slot, j], sem_ref.at[slot])


def _paged_pipeline(pt_ref, pools, bufs, sems, li, pp):
    b, c = pl.program_id(0), pl.program_id(1)
    nb, nc = pl.num_programs(0), pl.num_programs(1)
    step = b * nc + c
    slot = step % 2

    def fetch(b_, c_, slot_):
        for j in range(pp):
            page = pt_ref[b_, c_ * pp + j]
            for pool, buf, sem in zip(pools, bufs, sems):
                _page_copy(pool, buf, sem, li, page, slot_, j).start()

    @pl.when(step == 0)
    def _():
        fetch(0, 0, 0)

    @pl.when(step + 1 < nb * nc)
    def _():
        wrap = c + 1 == nc
        fetch(jnp.where(wrap, b + 1, b), jnp.where(wrap, 0, c + 1), 1 - slot)

    for pool, buf, sem in zip(pools, bufs, sems):
        pltpu.make_async_copy(pool.at[li, pl.ds(0, pp)], buf.at[slot], sem.at[slot]).wait()
    return slot


def _mla_sample_body(pt_ref, ql_ref, qr_ref, latn_ref, krn_ref, lat_hbm, kr_hbm, o_ref,
                     latbuf, krbuf, sems, m_ref, l_ref, acc_ref, *, pp, li):
    slot = _paged_pipeline(pt_ref, (lat_hbm, kr_hbm), (latbuf, krbuf), (sems.at[0], sems.at[1]), li, pp)
    pc = pl.program_id(1)

    @pl.when(pc == 0)
    def _():
        _flash_init(m_ref, l_ref, acc_ref)

    ql, qr = ql_ref[...], qr_ref[...]
    lats = [latbuf[slot, j].astype(BF16) for j in range(pp)]
    s = jnp.concatenate([_dot_nt(ql, lats[j]) + _dot(qr, krbuf[slot, j].astype(BF16)) for j in range(pp)], axis=1)

    def pv(p):
        acc = _dot(p[:, :PAGE], lats[0])
        for j in range(1, pp):
            acc = acc + _dot(p[:, j * PAGE:(j + 1) * PAGE], lats[j])
        return acc

    _online_update(s, None, pv, m_ref, l_ref, acc_ref)

    @pl.when(pc == pl.num_programs(1) - 1)
    def _():
        latn = _pad_rows16(latn_ref[...]).astype(BF16)
        sn = _dot_nt(ql, latn) + _dot_nt(qr, _pad_rows16(krn_ref[...]).astype(BF16))
        rows = sn.shape[0]
        tok = lax.broadcasted_iota(jnp.int32, (rows, NEW_PAD), 0) >> 3
        key = lax.broadcasted_iota(jnp.int32, (rows, NEW_PAD), 1)
        _online_update(sn, key <= tok, lambda p: _dot(p, latn), m_ref, l_ref, acc_ref)
        o_ref[...] = _flash_result(l_ref, acc_ref).astype(o_ref.dtype)


def _mla_sample(ql, qr, lat_new, kr_new, pool_lat, pool_kr, li, page_table, n_new):
    db, n_pages = page_table.shape
    pp = _pages_per_step(n_pages)
    rows = n_new * MLA_HEADS
    row_spec = lambda w: pl.BlockSpec((rows, w), lambda b, c, pt: (b, 0))
    new_spec = lambda w: pl.BlockSpec((n_new, w), lambda b, c, pt: (b, 0))
    hbm_spec = pl.BlockSpec(memory_space=pl.ANY)
    pool_kr = jnp.transpose(pool_kr, (0, 1, 3, 2))
    grid_spec = pltpu.PrefetchScalarGridSpec(
        num_scalar_prefetch=1, grid=(db, n_pages // pp),
        in_specs=[row_spec(KV_RANK), row_spec(ROPE), new_spec(KV_RANK), new_spec(ROPE), hbm_spec, hbm_spec],
        out_specs=row_spec(KV_RANK),
        scratch_shapes=[pltpu.VMEM((2, pp, PAGE, KV_RANK), F32), pltpu.VMEM((2, pp, ROPE, PAGE), F32),
                        pltpu.SemaphoreType.DMA((2, 2)),
                        pltpu.VMEM((rows, 1), F32), pltpu.VMEM((rows, 1), F32), pltpu.VMEM((rows, KV_RANK), F32)])
    return pl.pallas_call(
        functools.partial(_mla_sample_body, pp=pp, li=li), grid_spec=grid_spec, name="mla_sample",
        out_shape=jax.ShapeDtypeStruct((db * rows, KV_RANK), BF16),
        compiler_params=_cparams(("arbitrary", "arbitrary")))(
            page_table, ql, qr, lat_new, kr_new, pool_lat, pool_kr)


def _ssd_body(xbc_ref, dt_ref, dtt_ref, conv0_ref, h0_ref, cw_ref, cb_ref, alr_ref, alc_ref, dsk_ref,
              y_ref, convn_ref, hn_ref, xpad_ref, h_ref, *, q):
    c = pl.program_id(1)
    keep = CONV_W - 1

    @pl.when(c == 0)
    def _():
        h_ref[...] = h0_ref[...]
        xpad_ref[8 - keep:8, :] = conv0_ref[...]

    xpad_ref[8:8 + q, :] = xbc_ref[...]
    conv = cb_ref[...] + xpad_ref[8 - keep:8 - keep + q, :] * cw_ref[0:1, :]
    for w in range(1, CONV_W):
        conv = conv + xpad_ref[8 - keep + w:8 - keep + w + q, :] * cw_ref[w:w + 1, :]
    tail = xpad_ref[8 + q - keep:8 + q, :]
    convn_ref[...] = tail
    xpad_ref[8 - keep:8, :] = tail

    u = _silu(conv)
    xs = u[:, :D_INNER]
    gn = SSM_GROUPS * D_STATE
    bm, cm = u[:, D_INNER:D_INNER + gn], u[:, D_INNER + gn:]

    dt = dt_ref[...]
    dtt = dtt_ref[...]
    a_row = -jnp.exp(alr_ref[...])
    a_col = -jnp.exp(alc_ref[...])
    ii = lax.broadcasted_iota(jnp.int32, (q, q), 0)
    jj = lax.broadcasted_iota(jnp.int32, (q, q), 1)
    tril = ii >= jj
    cs = jnp.dot(tril.astype(F32), dt * a_row, precision=lax.Precision.HIGHEST, preferred_element_type=F32)
    cst = jnp.dot(dtt * a_col, (ii <= jj).astype(F32), precision=lax.Precision.HIGHEST, preferred_element_type=F32)
    rpg = SSM_HEADS // SSM_GROUPS
    for g in range(SSM_GROUPS):
        bg = bm[:, g * D_STATE:(g + 1) * D_STATE]
        cg = cm[:, g * D_STATE:(g + 1) * D_STATE].astype(BF16)
        cb = _dot_nt(cg, bg.astype(BF16))
        for r in range(rpg):
            h = g * rpg + r
            cs_col, cs_row = cs[:, h:h + 1], cst[h:h + 1, :]
            dt_col, dt_row = dt[:, h:h + 1], dtt[h:h + 1, :]
            cs_last = cst[h:h + 1, q - 1:q]
            decay = jnp.where(tril, jnp.exp(jnp.minimum(cs_col - cs_row, 0.0)), 0.0)
            lmat = (cb * decay * dt_row).astype(BF16)
            xs_h = xs[:, h * SSM_HEADDIM:(h + 1) * SSM_HEADDIM]
            xs_hb = xs_h.astype(BF16)
            y_diag = _dot(lmat, xs_hb)
            h_prev = h_ref[h]
            y_off = _dot_nt(cg, h_prev.astype(BF16)) * jnp.exp(cs_col)
            y_ref[:, h * SSM_HEADDIM:(h + 1) * SSM_HEADDIM] = y_diag + y_off + xs_h * dsk_ref[0:1, h:h + 1]
            bw = (bg * (jnp.exp(cs_last - cs_col) * dt_col)).astype(BF16)
            h_ref[h] = h_prev * jnp.exp(cs_last) + _dot_tn(xs_hb, bw)
    hn_ref[...] = h_ref[...]


def _ssd(xbc, dt, conv0, h0, conv_w, conv_b, a_log, d_skip, batch):
    rows = xbc.shape[0]
    length = rows // batch
    q = SSD_CHUNK if length % SSD_CHUNK == 0 else length
    nc = length // q
    dtt = jnp.transpose(dt.reshape(batch * nc, q, SSM_HEADS), (0, 2, 1))
    keep = CONV_W - 1
    in_specs = [pl.BlockSpec((q, CONV_DIM), lambda b, c: (b * nc + c, 0)),
                pl.BlockSpec((q, SSM_HEADS), lambda b, c: (b * nc + c, 0)),
                pl.BlockSpec((None, SSM_HEADS, q), lambda b, c: (b * nc + c, 0, 0)),
                pl.BlockSpec((None, keep, CONV_DIM), lambda b, c: (b, 0, 0)),
                pl.BlockSpec((None, SSM_HEADS, SSM_HEADDIM, D_STATE), lambda b, c: (b, 0, 0, 0)),
                _const_spec((CONV_W, CONV_DIM)), _const_spec((1, CONV_DIM)), _const_spec((1, SSM_HEADS)),
                _const_spec((SSM_HEADS, 1)), _const_spec((1, SSM_HEADS))]
    out_specs = [pl.BlockSpec((q, D_INNER), lambda b, c: (b * nc + c, 0)),
                 pl.BlockSpec((None, keep, CONV_DIM), lambda b, c: (b, 0, 0)),
                 pl.BlockSpec((None, SSM_HEADS, SSM_HEADDIM, D_STATE), lambda b, c: (b, 0, 0, 0))]
    out_shape = [jax.ShapeDtypeStruct((rows, D_INNER), F32), jax.ShapeDtypeStruct((batch, keep, CONV_DIM), F32),
                 jax.ShapeDtypeStruct((batch, SSM_HEADS, SSM_HEADDIM, D_STATE), F32)]
    return pl.pallas_call(
        functools.partial(_ssd_body, q=q), grid=(batch, nc), in_specs=in_specs, out_specs=out_specs,
        out_shape=out_shape, name="ssd",
        scratch_shapes=[pltpu.VMEM((8 + q, CONV_DIM), F32), pltpu.VMEM((SSM_HEADS, SSM_HEADDIM, D_STATE), F32)],
        compiler_params=_cparams(("parallel", "arbitrary")))(
            xbc, dt, dtt, conv0, h0, conv_w, conv_b.reshape(1, -1), a_log.reshape(1, -1), a_log.reshape(-1, 1),
            d_skip.reshape(1, -1))


CHUNK_W = CMP_STRIDE * NSA_GROUPS * NSA_D
KC_PAD = 16
NEAR = 32


def _prep_cmp(pe, w1, w2):
    eye = jnp.eye(NSA_GROUPS, dtype=w1.dtype)

    def big(w):
        return (w[:, None, :, None, :] * eye[None, :, None, :, None]).reshape(CHUNK_W, NSA_GROUPS * NSA_D)

    w_big = jnp.concatenate([big(w1[:CMP_STRIDE]), big(w1[CMP_STRIDE:])], axis=1).astype(BF16)
    peb = jnp.einsum('ld,ldh->h', pe, w1)
    peb2 = jnp.tile(peb, NSA_GROUPS).reshape(1, -1)
    w2_bd = (w2[None, :, None, :] * eye[:, None, :, None]).reshape(NSA_GROUPS * NSA_D, NSA_GROUPS * NSA_D).astype(BF16)
    return w_big, peb2, w2_bd


def _mm_body(x_ref, w_ref, o_ref):
    o_ref[...] = _dot(x_ref[...].astype(BF16), w_ref[...])


def _cmp_u_prompt(rows_kv, w_big):
    chunks = rows_kv.reshape(-1, CHUNK_W)
    (u,) = _tok_call(_mm_body, chunks.shape[0], [chunks], [w_big], [(w_big.shape[1], F32)])
    return u


def _feature_major(pool):
    return jnp.transpose(pool, (0, 1, 3, 4, 2))


def _cmp_u_paged_body(pt_ref, wk_ref, wv_ref, k_hbm, v_hbm, uk_ref, uv_ref, kbuf, vbuf, sems, tok_ref, *, pp, li):
    slot = _paged_pipeline(pt_ref, (k_hbm, v_hbm), (kbuf, vbuf), (sems.at[0], sems.at[1]), li, pp)
    feat = NSA_GROUPS * NSA_D
    cpp = PAGE // CMP_STRIDE
    for buf, w_ref, o_ref in ((kbuf, wk_ref, uk_ref), (vbuf, wv_ref, uv_ref)):
        for j in range(pp):
            tok_ref[j * PAGE:(j + 1) * PAGE, :] = buf[slot, j].reshape(feat, PAGE).T
        acc = None
        for l in range(CMP_STRIDE):
            x = tok_ref[pl.ds(l, pp * cpp, stride=CMP_STRIDE), :].astype(BF16)
            part = _dot(x, w_ref[l * feat:(l + 1) * feat, :])
            acc = part if acc is None else acc + part
        o_ref[...] = acc


def _cmp_u_paged(pool_k, pool_v, li, page_table, wk_big, wv_big):
    db, n_pages = page_table.shape
    pp = _pages_per_step(n_pages)
    cpp = PAGE // CMP_STRIDE
    width = wk_big.shape[1]
    hbm_spec = pl.BlockSpec(memory_space=pl.ANY)
    w_spec = pl.BlockSpec(wk_big.shape, lambda b, c, pt: (0, 0))
    u_spec = pl.BlockSpec((None, pp * cpp, width), lambda b, c, pt: (b, c, 0))
    page_buf = pltpu.VMEM((2, pp, NSA_GROUPS, NSA_D, PAGE), F32)
    grid_spec = pltpu.PrefetchScalarGridSpec(
        num_scalar_prefetch=1, grid=(db, n_pages // pp),
        in_specs=[w_spec, w_spec, hbm_spec, hbm_spec], out_specs=[u_spec, u_spec],
        scratch_shapes=[page_buf, page_buf, pltpu.SemaphoreType.DMA((2, 2)),
                        pltpu.VMEM((pp * PAGE, NSA_GROUPS * NSA_D), F32)])
    u_shape = jax.ShapeDtypeStruct((db, n_pages * cpp, width), F32)
    return pl.pallas_call(
        functools.partial(_cmp_u_paged_body, pp=pp, li=li), grid_spec=grid_spec, name="cmp_u_paged",
        out_shape=[u_shape, u_shape],
        compiler_params=_cparams(("arbitrary", "arbitrary")))(
            page_table, wk_big, wv_big, _feature_major(pool_k), _feature_major(pool_v))


def _cmp_finish_body(uk_ref, uv_ref, pk_ref, pv_ref, wk_ref, wv_ref, kc_ref, vc_ref, *, n):
    half = NSA_GROUPS * NSA_D

    def fin(u_ref, p_ref, w_ref, o_ref):
        u = u_ref[...]
        hid = u[:, :half] + pltpu.roll(u[:, half:], n - 1, 0) + p_ref[...]
        o_ref[...] = jnp.zeros(o_ref.shape, F32)
        o_ref[KC_PAD:KC_PAD + n, :] = _dot(_silu(hid).astype(BF16), w_ref[...])

    fin(uk_ref, pk_ref, wk_ref, kc_ref)
    fin(uv_ref, pv_ref, wv_ref, vc_ref)


def _cmp_finish(uk, uv, pk, pv, batch):
    n = uk.shape[0] // batch
    half = NSA_GROUPS * NSA_D
    n_out = KC_PAD + n + NEAR
    u_spec = pl.BlockSpec((n, 2 * half), lambda b: (b, 0))
    o_spec = pl.BlockSpec((None, n_out, half), lambda b: (b, 0, 0))
    consts = [pk[1], pv[1], pk[2], pv[2]]
    return pl.pallas_call(
        functools.partial(_cmp_finish_body, n=n), grid=(batch,), name="cmp_finish",
        in_specs=[u_spec, u_spec] + [_const_spec(a.shape) for a in consts], out_specs=[o_spec, o_spec],
        out_shape=[jax.ShapeDtypeStruct((batch, n_out, half), F32)] * 2,
        compiler_params=_cparams(("parallel",)))(uk, uv, *consts)


def _stack_heads(q, g):
    t = q.shape[0]
    lane = lax.broadcasted_iota(jnp.int32, (t, 128), 1)
    lo = lane < NSA_D
    parts = []
    for m in range(g * 4, g * 4 + 4):
        x = q[:, 128 * m:128 * (m + 1)]
        xr = pltpu.roll(x, NSA_D, 1)
        if g == 0:
            parts += [jnp.where(lo, x, 0.0), jnp.where(lo, xr, 0.0)]
        else:
            parts += [jnp.where(lo, 0.0, xr), jnp.where(lo, 0.0, x)]
    return jnp.concatenate(parts, axis=0)


def _unstack_heads(o, g, t):
    lane = lax.broadcasted_iota(jnp.int32, (t, 128), 1)
    lo = lane < NSA_D
    blocks = []
    for m in range(4):
        a, b = o[(2 * m) * t:(2 * m + 1) * t], o[(2 * m + 1) * t:(2 * m + 2) * t]
        if g == 0:
            blocks.append(jnp.where(lo, a, pltpu.roll(b, NSA_D, 1)))
        else:
            blocks.append(jnp.where(lo, pltpu.roll(a, NSA_D, 1), b))
    return blocks


def _bias_rows(rel_bias, dist):
    t, k = dist.shape
    b = rel_bias[jnp.asarray(_t5_bucket_np(dist))] * LOG2E
    return jnp.transpose(b, (2, 0, 1)).reshape(NSA_GROUPS, NSA_HPG * t, k).astype(F32)


def _far_bias(rel_bias, t):
    col = jnp.repeat(rel_bias[NUM_BUCKETS - 1] * LOG2E, t)
    return col.reshape(NSA_GROUPS, NSA_HPG * t, 1).astype(F32)


def _topk_mask(score, lane_f, n_rounds):
    sel = jnp.zeros(score.shape, F32)
    for _ in range(n_rounds):
        mx = jnp.max(score, axis=1, keepdims=True)
        idx = jnp.min(jnp.where(score == mx, lane_f, 1e9), axis=1, keepdims=True)
        pick = lane_f == idx
        sel = jnp.where(pick, 1.0, sel)
        score = jnp.where(pick, -2.0, score)
    return sel


def _select_scores(imp, tpos, n_slc):
    lane = lax.broadcasted_iota(jnp.int32, imp.shape, 1)
    cur = tpos >> 6
    forced = (lane == 0) | (lane == cur) | (lane == cur - 1)
    score = jnp.where(lane <= cur, jnp.where(forced, 1e30, imp), -1.0)
    score = jnp.where(lane < n_slc, score, -10.0)
    return score, lane.astype(F32)


def _cmp_prompt_body(q_ref, kc_ref, vc_ref, ov_ref, bn_ref, bf_ref, oc_ref, sel_ref, *, n_cmp, n_slc):
    i = pl.program_id(0)
    start = pl.multiple_of(i * (TQ // CMP_STRIDE), 8)
    n_all = kc_ref.shape[0]
    widths = [min(n_all, KC_PAD + CMP_FAR_STEP * (c + 1)) for c in range(-(-(n_all - KC_PAD) // CMP_FAR_STEP))]
    cls = jnp.minimum(jnp.maximum(start - KC_PAD - 1, 0) // CMP_FAR_STEP, len(widths) - 1)
    for c, width in enumerate(widths):
        @pl.when(cls == c)
        def _(width=width):
            _cmp_prompt_tile(q_ref, kc_ref, vc_ref, ov_ref, bn_ref, bf_ref, oc_ref, sel_ref, i, start, width,
                             n_cmp, n_slc)


CMP_FAR_STEP = 256


def _cmp_prompt_tile(q_ref, kc_ref, vc_ref, ov_ref, bn_ref, bf_ref, oc_ref, sel_ref, i, start, n_rows, n_cmp, n_slc):
    q = q_ref[...]
    kc_all, vc_all = kc_ref[0:n_rows, :].astype(BF16), vc_ref[0:n_rows, :].astype(BF16)
    kc_near = kc_ref[pl.ds(start, NEAR), :].astype(BF16)
    vc_near = vc_ref[pl.ds(start, NEAR), :].astype(BF16)
    ov_all = ov_ref[0:n_rows, :].astype(BF16)
    ov_near = ov_ref[pl.ds(start, NEAR), :].astype(BF16)
    rows = NSA_HPG * TQ
    colf = lax.broadcasted_iota(jnp.int32, (rows, n_rows), 1)
    mask_f = (colf >= KC_PAD) & (colf < start)
    t_n = lax.broadcasted_iota(jnp.int32, (rows, NEAR), 0) & (TQ - 1)
    jj = lax.broadcasted_iota(jnp.int32, (rows, NEAR), 1)
    jabs = start - KC_PAD + jj
    mask_n = (t_n - CMP_STRIDE * (jj - KC_PAD) - (CMP_LEN - 1) >= 0) & (jabs >= 0) & (jabs < n_cmp)
    tpos = i * TQ + lax.broadcasted_iota(jnp.int32, (TQ, 1), 0)
    for g in range(NSA_GROUPS):
        qg = _stack_heads(q, g).astype(BF16)
        s_f = jnp.where(mask_f, _dot_nt(qg, kc_all) + bf_ref[g], NEG)
        s_n = jnp.where(mask_n, _dot_nt(qg, kc_near) + bn_ref[g], NEG)
        m = jnp.maximum(jnp.max(s_f, axis=1, keepdims=True), jnp.max(s_n, axis=1, keepdims=True))
        p_f = jnp.where(mask_f, jnp.exp2(s_f - m), 0.0)
        p_n = jnp.where(mask_n, jnp.exp2(s_n - m), 0.0)
        l = jnp.sum(p_f, axis=1, keepdims=True) + jnp.sum(p_n, axis=1, keepdims=True)
        inv = 1.0 / jnp.where(l > 0.0, l, 1.0)
        p_f, p_n = p_f * inv, p_n * inv
        o = _dot(p_f.astype(BF16), vc_all) + _dot(p_n.astype(BF16), vc_near)
        for m4, blk in enumerate(_unstack_heads(o, g, TQ)):
            col = (g * 4 + m4) * 128
            oc_ref[:, col:col + 128] = blk
        ps_f, ps_n = p_f[:TQ], p_n[:TQ]
        for r in range(1, NSA_HPG):
            ps_f = ps_f + p_f[r * TQ:(r + 1) * TQ]
            ps_n = ps_n + p_n[r * TQ:(r + 1) * TQ]
        imp = _dot(ps_f.astype(BF16), ov_all) + _dot(ps_n.astype(BF16), ov_near)
        score, lane_f = _select_scores(imp, tpos, n_slc)
        picked = _topk_mask(score, lane_f, min(N_SEL, n_slc))
        sel_ref[g] = jnp.where(picked > 0.5, 0.0, NEG).astype(sel_ref.dtype)


def _overlap_np(n_rows, n_cmp, n_slc, width):
    ov = np.zeros((n_rows, width), np.float32)
    cs = np.arange(n_cmp)[:, None] * CMP_STRIDE
    bs = np.arange(n_slc)[None, :] * SLC_BLOCK
    ov[KC_PAD:KC_PAD + n_cmp, :n_slc] = np.maximum(
        np.minimum(cs + CMP_LEN, bs + SLC_BLOCK) - np.maximum(cs, bs), 0)
    return ov


def _cmp_prompt(q, kc, vc, rel_bias, seq):
    n_rows = kc.shape[0]
    n_cmp = (seq - CMP_LEN) // CMP_STRIDE + 1
    n_slc = -(-seq // SLC_BLOCK)
    w_slc = -(-n_slc // 128) * 128
    ov = jnp.asarray(_overlap_np(n_rows, n_cmp, n_slc, w_slc))
    t = np.arange(TQ)[:, None]
    jj = np.arange(NEAR)[None, :]
    b_near = _bias_rows(rel_bias, t - CMP_STRIDE * (jj - KC_PAD) - (CMP_LEN - 1))
    b_far = _far_bias(rel_bias, TQ)
    rows = NSA_HPG * TQ
    return pl.pallas_call(
        functools.partial(_cmp_prompt_body, n_cmp=n_cmp, n_slc=n_slc), grid=(seq // TQ,), name="cmp_prompt",
        in_specs=[pl.BlockSpec((TQ, 1024), lambda i: (i, 0)), _const_spec(kc.shape), _const_spec(vc.shape),
                  _const_spec(ov.shape), _const_spec(b_near.shape), _const_spec(b_far.shape)],
        out_specs=[pl.BlockSpec((TQ, 1024), lambda i: (i, 0)), pl.BlockSpec((NSA_GROUPS, TQ, w_slc), lambda i: (0, i, 0))],
        out_shape=[jax.ShapeDtypeStruct((seq, 1024), F32), jax.ShapeDtypeStruct((NSA_GROUPS, seq, w_slc), BF16)],
        compiler_params=_cparams(("parallel",)))(q, kc, vc, ov, b_near, b_far)


def _cmp_sample_body(q_ref, kc_ref, vc_ref, ov_ref, bias_ref, oc_ref, sel_ref, *, n_cmp, n_slc, past, t):
    q = q_ref[...]
    kc_all, vc_all = kc_ref[...].astype(BF16), vc_ref[...].astype(BF16)
    ov_all = ov_ref[...].astype(BF16)
    n_rows = kc_all.shape[0]
    rows = NSA_HPG * t
    col = lax.broadcasted_iota(jnp.int32, (rows, n_rows), 1)
    tok = lax.broadcasted_iota(jnp.int32, (rows, n_rows), 0) & (t - 1)
    j = col - KC_PAD
    mask = (j >= 0) & (j < n_cmp) & (past + tok - CMP_STRIDE * j - (CMP_LEN - 1) >= 0)
    tpos = past + lax.broadcasted_iota(jnp.int32, (t, 1), 0)
    for g in range(NSA_GROUPS):
        qg = _stack_heads(q, g).astype(BF16)
        s = jnp.where(mask, _dot_nt(qg, kc_all) + bias_ref[g], NEG)
        m = jnp.max(s, axis=1, keepdims=True)
        p = jnp.where(mask, jnp.exp2(s - m), 0.0)
        l = jnp.sum(p, axis=1, keepdims=True)
        p = p * (1.0 / jnp.where(l > 0.0, l, 1.0))
        o = _dot(p.astype(BF16), vc_all)
        for m4, blk in enumerate(_unstack_heads(o, g, t)):
            c0 = (g * 4 + m4) * 128
            oc_ref[:, c0:c0 + 128] = blk
        ps = p[:t]
        for r in range(1, NSA_HPG):
            ps = ps + p[r * t:(r + 1) * t]
        imp = _dot(ps.astype(BF16), ov_all)
        score, lane_f = _select_scores(imp, tpos, n_slc)
        sel_ref[g] = _topk_mask(score, lane_f, min(N_SEL, n_slc)).astype(sel_ref.dtype)


def _cmp_sample(q, kc, vc, rel_bias, past, t):
    db, n_rows = kc.shape[0], kc.shape[1]
    total = past + t
    n_cmp = (total - CMP_LEN) // CMP_STRIDE + 1
    n_slc = -(-total // SLC_BLOCK)
    w_slc = -(-n_slc // 128) * 128
    ov = jnp.asarray(_overlap_np(n_rows, n_cmp, n_slc, w_slc))
    tt = np.arange(t)[:, None]
    jn = np.arange(n_rows)[None, :] - KC_PAD
    bias = _bias_rows(rel_bias, past + tt - CMP_STRIDE * jn - (CMP_LEN - 1))
    return pl.pallas_call(
        functools.partial(_cmp_sample_body, n_cmp=n_cmp, n_slc=n_slc, past=past, t=t), grid=(db,), name="cmp_sample",
        in_specs=[pl.BlockSpec((t, 1024), lambda b: (b, 0)), pl.BlockSpec((None, n_rows, 128), lambda b: (b, 0, 0)),
                  pl.BlockSpec((None, n_rows, 128), lambda b: (b, 0, 0)), _const_spec(ov.shape), _const_spec(bias.shape)],
        out_specs=[pl.BlockSpec((t, 1024), lambda b: (b, 0)),
                   pl.BlockSpec((None, NSA_GROUPS, t, w_slc), lambda b: (b, 0, 0, 0))],
        out_shape=[jax.ShapeDtypeStruct((db * t, 1024), F32), jax.ShapeDtypeStruct((db, NSA_GROUPS, t, w_slc), F32)],
        compiler_params=_cparams(("parallel",)))(q, kc, vc, ov, bias)


def _sum_lane(g):
    return NSA_D * (1 - g)


def _slc_prompt_body(qi_ref, ki_ref, q_ref, ks_ref, vs_ref, sel_ref, place_ref, ep_ref, br_ref, o_ref,
                     qs_ref, m_ref, acc_ref, *, tk):
    step = pl.program_id(0)
    qi, ki = qi_ref[step], ki_ref[step]
    rows = NSA_HPG * TQ
    sub = tk // TQ

    @pl.when(ki == 0)
    def _():
        q = q_ref[...]
        for g in range(NSA_GROUPS):
            qs_ref[g] = _stack_heads(q, g).astype(BF16)
        m_ref[...] = jnp.full(m_ref.shape, NEG, F32)
        acc_ref[...] = jnp.zeros(acc_ref.shape, F32)

    near = (ki * sub + sub - 1) >= qi - 1

    def run(is_near):
        ks, vs = ks_ref[...], vs_ref[...]
        lane = lax.broadcasted_iota(jnp.int32, (tk, LANES), 1)
        for g in range(NSA_GROUPS):
            other = (lane >= NSA_D) if g == 0 else (lane < NSA_D)
            k_aug = jnp.where(other, ep_ref[g], ks)
            v_aug = jnp.where(lane == _sum_lane(g), jnp.ones((), BF16), vs)
            selpad = _dot(sel_ref[g], place_ref[g]).astype(BF16)
            q_aug = (qs_ref[g].reshape(NSA_HPG, TQ, LANES) + selpad[None]).reshape(rows, LANES)
            ts = tk // 2 if tk >= 2 * TQ else tk
            for h in range(tk // ts):
                s = _dot_nt(q_aug, k_aug[h * ts:(h + 1) * ts])
                if is_near:
                    tok = qi * TQ - ki * tk - h * ts + lax.broadcasted_iota(jnp.int32, (TQ, ts), 0)
                    causal = jnp.where(lax.broadcasted_iota(jnp.int32, (TQ, ts), 1) <= tok, 0.0, NEG)
                    s = (s.reshape(NSA_HPG, TQ, ts) + causal[None]).reshape(rows, ts)
                    pieces = []
                    for u in range(h * ts // TQ, (h + 1) * ts // TQ):
                        delta = qi - (ki * sub + u)
                        pieces.append(jnp.where(delta == 0, br_ref[g, 0], jnp.where(delta == 1, br_ref[g, 1], 0.0)))
                    s = s + jnp.concatenate(pieces, axis=1)
                m_prev = m_ref[g]
                m_new = jnp.maximum(m_prev, jnp.max(s, axis=1, keepdims=True))
                p = jnp.exp2((s - _lane_tile(m_new, ts // LANES)).astype(BF16))
                acc_ref[g] = acc_ref[g] * jnp.exp2(m_prev - m_new) + _dot(p, v_aug[h * ts:(h + 1) * ts])
                m_ref[g] = m_new

    @pl.when(near)
    def _():
        run(True)

    @pl.when(jnp.logical_not(near))
    def _():
        run(False)

    @pl.when(ki == (qi * TQ + TQ - 1) // tk)
    def _():
        for g in range(NSA_GROUPS):
            acc = acc_ref[g]
            o = acc * (1.0 / acc[:, _sum_lane(g):_sum_lane(g) + 1])
            for m4, blk in enumerate(_unstack_heads(o, g, TQ)):
                col = (g * 4 + m4) * 128
                o_ref[:, col:col + 128] = blk


def _tile_bias(rel_bias):
    t = np.arange(TQ)[:, None]
    c = np.arange(TQ)[None, :]
    return jnp.stack([_bias_rows(rel_bias, d * TQ + t - c) for d in (0, 1)], axis=1)


def _slc_prompt(q, ksb, vsb, sel_neg, rel_bias, seq):
    tk = _key_tile(seq, 1024)
    rows = NSA_HPG * TQ
    nk = seq // tk
    bpt = tk // SLC_BLOCK
    qi, ki = _causal_schedule(seq // TQ, TQ, tk)
    b_rel = _tile_bias(rel_bias) - _far_bias(rel_bias, TQ)[:, None]
    sel_t = jnp.transpose(sel_neg[:, :, :nk * bpt].reshape(NSA_GROUPS, seq, nk, bpt), (0, 2, 1, 3))
    assert bpt <= NSA_D
    place = np.zeros((NSA_GROUPS, bpt, LANES), np.float32)
    onehot = np.zeros((NSA_GROUPS, tk, LANES), np.float32)
    for g in range(NSA_GROUPS):
        place[g, np.arange(bpt), _sum_lane(g) + np.arange(bpt)] = 1.0
        onehot[g, np.arange(tk), _sum_lane(g) + np.arange(tk) // SLC_BLOCK] = 1.0
    place, onehot = jnp.asarray(place).astype(BF16), jnp.asarray(onehot).astype(BF16)
    grid_spec = pltpu.PrefetchScalarGridSpec(
        num_scalar_prefetch=2, grid=(qi.shape[0],),
        in_specs=[pl.BlockSpec((TQ, 1024), lambda s, a, b: (a[s], 0)),
                  pl.BlockSpec((tk, 128), lambda s, a, b: (b[s], 0)),
                  pl.BlockSpec((tk, 128), lambda s, a, b: (b[s], 0)),
                  pl.BlockSpec((NSA_GROUPS, None, TQ, bpt), lambda s, a, b: (0, b[s], a[s], 0)),
                  pl.BlockSpec(place.shape, lambda s, a, b: (0, 0, 0)),
                  pl.BlockSpec(onehot.shape, lambda s, a, b: (0, 0, 0)),
                  pl.BlockSpec(b_rel.shape, lambda s, a, b: (0, 0, 0, 0))],
        out_specs=pl.BlockSpec((TQ, 1024), lambda s, a, b: (a[s], 0)),
        scratch_shapes=[pltpu.VMEM((NSA_GROUPS, rows, 128), BF16), pltpu.VMEM((NSA_GROUPS, rows, LANES), F32),
                        pltpu.VMEM((NSA_GROUPS, rows, 128), F32)])
    return pl.pallas_call(
        functools.partial(_slc_prompt_body, tk=tk), grid_spec=grid_spec, name="slc_prompt",
        out_shape=jax.ShapeDtypeStruct((seq, 1024), F32),
        compiler_params=_cparams(("arbitrary",)))(qi, ki, q, ksb, vsb, sel_t, place, onehot, b_rel)


N_WIN_BLK = WINDOW // TQ + 1


def _win_prompt_body(q_ref, *refs):
    k_refs, v_refs = refs[:N_WIN_BLK], refs[N_WIN_BLK:2 * N_WIN_BLK]
    bn_ref, bf_ref, o_ref = refs[2 * N_WIN_BLK:]
    i = pl.program_id(0)
    q = q_ref[...]
    rows = NSA_HPG * TQ
    n_keys = N_WIN_BLK * TQ
    k = jnp.concatenate([r[...] for r in k_refs], axis=0)
    v = jnp.concatenate([r[...] for r in v_refs], axis=0)
    t = lax.broadcasted_iota(jnp.int32, (rows, n_keys), 0) & (TQ - 1)
    c = lax.broadcasted_iota(jnp.int32, (rows, n_keys), 1)
    dist = WINDOW + t - c
    mask = (dist >= 0) & (dist < WINDOW) & ((i - (N_WIN_BLK - 1)) * TQ + c >= 0)
    for g in range(NSA_GROUPS):
        qg = _stack_heads(q, g).astype(BF16)
        far = jnp.broadcast_to(bf_ref[g], (rows, n_keys - 2 * TQ))
        bias = jnp.concatenate([far, bn_ref[g, 1], bn_ref[g, 0]], axis=1)
        s = jnp.where(mask, _dot_nt(qg, k) + bias, NEG)
        m = jnp.max(s, axis=1, keepdims=True)
        p = jnp.exp2(s - m)
        o = _dot(p.astype(BF16), v) / jnp.sum(p, axis=1, keepdims=True)
        for m4, blk in enumerate(_unstack_heads(o, g, TQ)):
            col = (g * 4 + m4) * 128
            o_ref[:, col:col + 128] = blk


def _win_prompt(q, kwb, vwb, rel_bias, seq):
    b_near = _tile_bias(rel_bias)
    b_far = _far_bias(rel_bias, TQ)

    def kv_spec(u):
        return pl.BlockSpec((TQ, 128), lambda i, _u=u: (jnp.maximum(i - (N_WIN_BLK - 1) + _u, 0), 0))

    return pl.pallas_call(
        _win_prompt_body, grid=(seq // TQ,), name="win_prompt",
        in_specs=[pl.BlockSpec((TQ, 1024), lambda i: (i, 0))] + [kv_spec(u) for u in range(N_WIN_BLK)] * 2
        + [_const_spec(b_near.shape), _const_spec(b_far.shape)],
        out_specs=pl.BlockSpec((TQ, 1024), lambda i: (i, 0)),
        out_shape=jax.ShapeDtypeStruct((seq, 1024), F32),
        compiler_params=_cparams(("parallel",)))(q, *([kwb] * N_WIN_BLK), *([vwb] * N_WIN_BLK), b_near, b_far)


def _slc_sample_body(pt_ref, q_ref, sel_ref, seln_ref, ex_ref, bias_ref, bnew_ref, kn_ref, vn_ref, k_hbm, v_hbm,
                     o_ref, kbuf, vbuf, sems, qs_ref, m_ref, l_ref, acc_ref, *, pp, t, li):
    slot = _paged_pipeline(pt_ref, (k_hbm, v_hbm), (kbuf, vbuf), (sems.at[0], sems.at[1]), li, pp)
    pc = pl.program_id(1)
    rows = NSA_HPG * t
    tk = pp * PAGE

    @pl.when(pc == 0)
    def _():
        q = q_ref[...]
        for g in range(NSA_GROUPS):
            qs_ref[g] = _stack_heads(q, g).astype(BF16)
            _flash_init(m_ref.at[g], l_ref.at[g], acc_ref.at[g])

    feat = NSA_GROUPS * NSA_D
    ks = [kbuf[slot, j].reshape(feat, PAGE).astype(BF16) for j in range(pp)]
    vs = [vbuf[slot, j].reshape(feat, PAGE).astype(BF16) for j in range(pp)]
    ex = ex_ref[...]
    for g in range(NSA_GROUPS):
        qg = qs_ref[g]
        s = jnp.concatenate([_dot(qg, ks[j]) for j in range(pp)], axis=1) + bias_ref[g]
        selx = _dot(sel_ref[g].astype(BF16), ex)
        mask = jnp.concatenate([selx] * NSA_HPG, axis=0) > 0.5

        def pv(p):
            acc = _dot_nt(p[:, :PAGE], vs[0])
            for j in range(1, pp):
                acc = acc + _dot_nt(p[:, j * PAGE:(j + 1) * PAGE], vs[j])
            return acc

        _online_update(s, mask, pv, m_ref.at[g], l_ref.at[g], acc_ref.at[g])

    @pl.when(pc == pl.num_programs(1) - 1)
    def _():
        kn, vn = _pad_rows16(kn_ref[...]).astype(BF16), _pad_rows16(vn_ref[...]).astype(BF16)
        tok = lax.broadcasted_iota(jnp.int32, (rows, NEW_PAD), 0) & (t - 1)
        key = lax.broadcasted_iota(jnp.int32, (rows, NEW_PAD), 1)
        for g in range(NSA_GROUPS):
            s = _dot_nt(qs_ref[g], kn) + bnew_ref[g]
            seln = jnp.concatenate([seln_ref[g]] * NSA_HPG, axis=0) > 0.5
            _online_update(s, seln & (key <= tok), lambda p: _dot(p, vn), m_ref.at[g], l_ref.at[g], acc_ref.at[g])
            o = _flash_result(l_ref.at[g], acc_ref.at[g])
            for m4, blk in enumerate(_unstack_heads(o, g, t)):
                col = (g * 4 + m4) * 128
                o_ref[:, col:col + 128] = blk


def _slc_sample(q, sel, ks_new, vs_new, pool_k, pool_v, li, page_table, rel_bias, t):
    db, n_pages = page_table.shape
    past = n_pages * PAGE
    pp = _pages_per_step(n_pages)
    n_steps = n_pages // pp
    tk = pp * PAGE
    bpt = tk // SLC_BLOCK
    rows = NSA_HPG * t
    sel_past = sel[..., :n_steps * bpt].reshape(db, NSA_GROUPS, t, n_steps, bpt)
    sel_past = jnp.transpose(sel_past, (0, 3, 1, 2, 4))
    new_blk = past // SLC_BLOCK
    sel_new = jnp.broadcast_to(sel[..., new_blk:new_blk + 1], (db, NSA_GROUPS, t, NEW_PAD))
    expand = jnp.asarray((np.arange(bpt)[:, None] == np.arange(tk)[None, :] // SLC_BLOCK).astype(np.float32)).astype(BF16)
    tt = np.arange(t)[:, None]
    assert tk >= FAR_DIST
    last_keys = (n_steps - 1) * tk + np.arange(tk)[None, :]
    bias = jnp.stack([jnp.broadcast_to(_far_bias(rel_bias, t), (NSA_GROUPS, rows, tk)),
                      _bias_rows(rel_bias, past + tt - last_keys)])
    b_new = _bias_rows(rel_bias, tt - np.arange(NEW_PAD)[None, :])
    hbm_spec = pl.BlockSpec(memory_space=pl.ANY)
    page_buf = pltpu.VMEM((2, pp, NSA_GROUPS, NSA_D, PAGE), F32)
    grid_spec = pltpu.PrefetchScalarGridSpec(
        num_scalar_prefetch=1, grid=(db, n_steps),
        in_specs=[pl.BlockSpec((t, 1024), lambda b, c, pt: (b, 0)),
                  pl.BlockSpec((None, None, NSA_GROUPS, t, bpt), lambda b, c, pt: (b, c, 0, 0, 0)),
                  pl.BlockSpec((None, NSA_GROUPS, t, NEW_PAD), lambda b, c, pt: (b, 0, 0, 0)),
                  pl.BlockSpec(expand.shape, lambda b, c, pt: (0, 0)),
                  pl.BlockSpec((None, NSA_GROUPS, rows, tk),
                               lambda b, c, pt: (jnp.where(c == n_steps - 1, 1, 0), 0, 0, 0)),
                  pl.BlockSpec(b_new.shape, lambda b, c, pt: (0, 0, 0)),
                  pl.BlockSpec((t, 128), lambda b, c, pt: (b, 0)),
                  pl.BlockSpec((t, 128), lambda b, c, pt: (b, 0)), hbm_spec, hbm_spec],
        out_specs=pl.BlockSpec((t, 1024), lambda b, c, pt: (b, 0)),
        scratch_shapes=[page_buf, page_buf, pltpu.SemaphoreType.DMA((2, 2)),
                        pltpu.VMEM((NSA_GROUPS, rows, 128), BF16), pltpu.VMEM((NSA_GROUPS, rows, 1), F32),
                        pltpu.VMEM((NSA_GROUPS, rows, 1), F32), pltpu.VMEM((NSA_GROUPS, rows, 128), F32)])
    return pl.pallas_call(
        functools.partial(_slc_sample_body, pp=pp, t=t, li=li), grid_spec=grid_spec, name="slc_sample",
        out_shape=jax.ShapeDtypeStruct((db * t, 1024), F32),
        compiler_params=_cparams(("arbitrary", "arbitrary")))(
            page_table, q, sel_past, sel_new, expand, bias, b_new, ks_new, vs_new,
            _feature_major(pool_k), _feature_major(pool_v))


def _win_sample_body(q_ref, wk_ref, wv_ref, kn_ref, vn_ref, bias_ref, o_ref, *, t, w_buf):
    q = q_ref[...]
    rows = NSA_HPG * t
    feat = NSA_GROUPS * NSA_D
    kt = wk_ref[...].reshape(feat, w_buf).astype(BF16)
    vt = wv_ref[...].reshape(feat, w_buf).astype(BF16)
    kn, vn = _pad_rows16(kn_ref[...]).astype(BF16), _pad_rows16(vn_ref[...]).astype(BF16)
    n_keys = w_buf + NEW_PAD
    tok = lax.broadcasted_iota(jnp.int32, (rows, n_keys), 0) & (t - 1)
    c = lax.broadcasted_iota(jnp.int32, (rows, n_keys), 1)
    dist = w_buf + tok - c
    mask = (dist >= 0) & (dist < WINDOW)
    for g in range(NSA_GROUPS):
        qg = _stack_heads(q, g).astype(BF16)
        s = jnp.concatenate([_dot(qg, kt), _dot_nt(qg, kn)], axis=1) + bias_ref[g]
        s = jnp.where(mask, s, NEG)
        m = jnp.max(s, axis=1, keepdims=True)
        p = jnp.exp2(s - m)
        pb = p.astype(BF16)
        o = (_dot_nt(pb[:, :w_buf], vt) + _dot(pb[:, w_buf:], vn)) / jnp.sum(p, axis=1, keepdims=True)
        for m4, blk in enumerate(_unstack_heads(o, g, t)):
            col = (g * 4 + m4) * 128
            o_ref[:, col:col + 128] = blk


def _win_sample(q, wk_buf, wv_buf, kw_new, vw_new, rel_bias, t):
    db, w_buf = wk_buf.shape[0], wk_buf.shape[1]
    tt = np.arange(t)[:, None]
    bias = _bias_rows(rel_bias, w_buf + tt - np.arange(w_buf + NEW_PAD)[None, :])
    wk = jnp.transpose(wk_buf, (0, 2, 3, 1))
    wv = jnp.transpose(wv_buf, (0, 2, 3, 1))
    buf_spec = pl.BlockSpec((None, NSA_GROUPS, NSA_D, w_buf), lambda b: (b, 0, 0, 0))
    return pl.pallas_call(
        functools.partial(_win_sample_body, t=t, w_buf=w_buf), grid=(db,), name="win_sample",
        in_specs=[pl.BlockSpec((t, 1024), lambda b: (b, 0)), buf_spec, buf_spec, pl.BlockSpec((t, 128), lambda b: (b, 0)),
                  pl.BlockSpec((t, 128), lambda b: (b, 0)), _const_spec(bias.shape)],
        out_specs=pl.BlockSpec((t, 1024), lambda b: (b, 0)),
        out_shape=jax.ShapeDtypeStruct((db * t, 1024), F32),
        compiler_params=_cparams(("parallel",)))(q, wk, wv, kw_new, vw_new, bias)


def kernel(x_prompt, x_sample, cache_mla_latent, cache_mla_krope, state_ssm, state_conv, cache_cmp_k, cache_cmp_v, cache_slc_k, cache_slc_v, state_win_k, state_win_v, page_table, rel_bias, w_in_a, q_norm, w_uq, kv_norm, w_uk, w_uv, conv_w, conv_b, dt_bias, a_log, d_skip, ssm_norm, w_out_a, ln_a_g, ln_a_b, w_in_c, cmp_pe_k, cmp_w1_k, cmp_w2_k, cmp_pe_v, cmp_w1_v, cmp_w2_v, w_out_c, ln_c_g, ln_c_b):
    bp, sp = x_prompt.shape[:2]
    db, ss = x_sample.shape[:2]
    n_pages = page_table.shape[1]
    past = n_pages * PAGE
    assert bp == 1 and x_prompt.shape[2] == D_MODEL and cache_mla_latent.shape[2] == PAGE
    assert sp % TQ == 0 and sp >= WINDOW and ss == 8
    assert past % SLC_BLOCK == 0 and ss < CMP_STRIDE and past % CMP_STRIDE == 0 and past >= WINDOW
    assert state_win_k.shape[2] == WINDOW

    hp = x_prompt.reshape(sp, D_MODEL)
    hs = x_sample.reshape(db * ss, D_MODEL)
    pos_p = jnp.arange(sp)
    pos_s = jnp.tile(past + jnp.arange(ss), db)
    st = {}

    pa = _prep_a(w_in_a[0], q_norm[0], w_uq[0], kv_norm[0], w_uk[0], w_uv[0], dt_bias[0], w_out_a[0])
    qlat, qrope, lat, latb, kr, krb, g, z, xbc, dt = _a_front(hp, pos_p, pa)
    o_lat = _mla_prompt(qlat.reshape(sp * MLA_HEADS, KV_RANK), qrope.reshape(sp * MLA_HEADS, ROPE), latb, krb)
    y, cbuf, hst = _ssd(xbc, dt, jnp.zeros((1, CONV_W - 1, CONV_DIM), F32),
                        jnp.zeros((1, SSM_HEADS, SSM_HEADDIM, D_STATE), F32), conv_w[0], conv_b[0], a_log[0],
                        d_skip[0], 1)
    hp = _a_back(hp, o_lat.reshape(sp, MLA_HEADS * KV_RANK), g, y, z, pa, ssm_norm[0], ln_a_g[0], ln_a_b[0])
    st['p_lat'], st['p_krope'] = lat.reshape(1, 1, sp, KV_RANK), kr.reshape(1, 1, sp, ROPE)
    st['p_ssm'], st['p_conv'] = hst[None], cbuf[None]
    qlat, qrope, lat, latb, kr, krb, g, z, xbc, dt = _a_front(hs, pos_s, pa)
    o_lat = _mla_sample(qlat.reshape(db * ss * MLA_HEADS, KV_RANK), qrope.reshape(db * ss * MLA_HEADS, ROPE), lat, kr,
                        cache_mla_latent, cache_mla_krope, 0, page_table, ss)
    y, cbuf, hst = _ssd(xbc, dt, state_conv[0], state_ssm[0], conv_w[0], conv_b[0], a_log[0], d_skip[0], db)
    hs = _a_back(hs, o_lat.reshape(db * ss, MLA_HEADS * KV_RANK), g, y, z, pa, ssm_norm[0], ln_a_g[0], ln_a_b[0])
    st['s_lat'], st['s_krope'] = lat.reshape(1, db, ss, KV_RANK), kr.reshape(1, db, ss, ROPE)
    st['s_ssm'], st['s_conv'] = hst[None], cbuf[None]

    pc = _prep_c(w_in_c[0], w_out_c[0])
    pk = _prep_cmp(cmp_pe_k[0], cmp_w1_k[0], cmp_w2_k[0])
    pv = _prep_cmp(cmp_pe_v[0], cmp_w1_v[0], cmp_w2_v[0])
    kv5 = lambda a, b_, s_: a.reshape(1, b_, s_, NSA_GROUPS, NSA_D)
    q, kc_r, vc_r, ks, vs, kw, vw, ksb, vsb, kwb, vwb, og = _c_front(hp, pc)
    kc, vc = _cmp_finish(_cmp_u_prompt(kc_r, pk[0]), _cmp_u_prompt(vc_r, pv[0]), pk, pv, 1)
    o_c, sel = _cmp_prompt(q, kc[0], vc[0], rel_bias, sp)
    o_s = _slc_prompt(q, ksb, vsb, sel, rel_bias, sp)
    o_w = _win_prompt(q, kwb, vwb, rel_bias, sp)
    hp = _c_back(hp, o_c, o_s, o_w, og, pc, ln_c_g[0], ln_c_b[0])
    keep = min(WINDOW, sp)
    for n, a in (('p_cmp_k', kc_r), ('p_cmp_v', vc_r), ('p_slc_k', ks), ('p_slc_v', vs)):
        st[n] = kv5(a, 1, sp)
    st['p_win_k'], st['p_win_v'] = kv5(kw[sp - keep:], 1, keep), kv5(vw[sp - keep:], 1, keep)
    q, kc_r, vc_r, ks, vs, kw, vw, ksb, vsb, kwb, vwb, og = _c_front(hs, pc)
    uk, uv = _cmp_u_paged(cache_cmp_k, cache_cmp_v, 0, page_table, pk[0], pv[0])
    n_chunks = uk.shape[1]
    kc, vc = _cmp_finish(uk.reshape(db * n_chunks, -1), uv.reshape(db * n_chunks, -1), pk, pv, db)
    o_c, sel = _cmp_sample(q, kc, vc, rel_bias, past, ss)
    o_s = _slc_sample(q, sel, ks, vs, cache_slc_k, cache_slc_v, 0, page_table, rel_bias, ss)
    o_w = _win_sample(q, state_win_k[0], state_win_v[0], kw, vw, rel_bias, ss)
    hs = _c_back(hs, o_c, o_s, o_w, og, pc, ln_c_g[0], ln_c_b[0])
    for n, a in (('s_cmp_k', kc_r), ('s_cmp_v', vc_r), ('s_slc_k', ks), ('s_slc_v', vs)):
        st[n] = kv5(a, db, ss)
    new5 = lambda a: a.reshape(db, ss, NSA_GROUPS, NSA_D)
    st['s_win_k'] = jnp.concatenate([state_win_k[0], new5(kw)], 1)[None, :, ss:]
    st['s_win_v'] = jnp.concatenate([state_win_v[0], new5(vw)], 1)[None, :, ss:]

    names = ['p_lat', 'p_krope', 'p_ssm', 'p_conv', 'p_cmp_k', 'p_cmp_v', 'p_slc_k', 'p_slc_v', 'p_win_k', 'p_win_v',
             's_lat', 's_krope', 's_ssm', 's_conv', 's_cmp_k', 's_cmp_v', 's_slc_k', 's_slc_v', 's_win_k', 's_win_v']
    return (hp.reshape(bp, sp, D_MODEL), hs.reshape(db, ss, D_MODEL)) + tuple(st[n] for n in names)
```

```python
import functools
import math

import numpy as np
import jax
import jax.numpy as jnp
from jax import lax
from jax.experimental import pallas as pl
from jax.experimental.pallas import tpu as pltpu

F32 = jnp.float32
BF16 = jnp.bfloat16

D_MODEL = 1024
DEPTH = 2
DEEPNORM_ALPHA = (2.0 * DEPTH) ** 0.25
LN_EPS = 1e-5
RMS_EPS = 1e-6
NEG = -1e30
PAGE = 128

MLA_HEADS = 8
Q_RANK = 384
KV_RANK = 256
NOPE = 64
ROPE = 32
MLA_V = 64
MLA_WIDTH = MLA_HEADS * MLA_V
LOG2E = math.log2(math.e)
MLA_SCALE = (NOPE + ROPE) ** -0.5 * LOG2E
ROPE_BASE = 10000.0

SSM_HEADDIM = 64
D_INNER = 1024
SSM_HEADS = 16
SSM_GROUPS = 2
D_STATE = 128
CONV_W = 4
CONV_DIM = D_INNER + 2 * SSM_GROUPS * D_STATE
SSD_CHUNK = 128

NSA_HEADS = 16
NSA_GROUPS = 2
NSA_HPG = 8
NSA_D = 64
NSA_SCALE = NSA_D ** -0.5 * LOG2E
CMP_LEN = 32
CMP_STRIDE = 16
SLC_BLOCK = 64
N_SEL = 16
WINDOW = 512
NUM_BUCKETS = 32
MAX_DISTANCE = 128
FAR_DIST = 128

TQ = 128
VMEM_LIMIT = 56 * 1024 * 1024


def _cparams(sem):
    return pltpu.CompilerParams(dimension_semantics=sem, vmem_limit_bytes=VMEM_LIMIT)


def _const_spec(shape):
    nd = len(shape)
    return pl.BlockSpec(shape, lambda *a, _nd=nd: (0,) * _nd)


def _dot(a, b):
    return jnp.dot(a, b, preferred_element_type=F32)


def _dot_nt(a, b):
    return lax.dot_general(a, b, (((1,), (1,)), ((), ())), preferred_element_type=F32)


def _dot_tn(a, b):
    return lax.dot_general(a, b, (((0,), (0,)), ((), ())), preferred_element_type=F32)


def _sigmoid(x):
    return 1.0 / (1.0 + jnp.exp(-x))


def _silu(x):
    return x * _sigmoid(x)


def _rms(x, g):
    return x * lax.rsqrt(jnp.mean(x * x, axis=-1, keepdims=True) + RMS_EPS) * g


def _deepnorm_ln(x, out, g, b):
    h = DEEPNORM_ALPHA * x + out
    mu = jnp.mean(h, axis=-1, keepdims=True)
    d = h - mu
    var = jnp.mean(d * d, axis=-1, keepdims=True)
    return d * lax.rsqrt(var + LN_EPS) * g + b


def _t5_bucket_np(dist):
    n = np.maximum(dist, 0)
    exact = NUM_BUCKETS // 2
    nf = np.maximum(n, exact).astype(np.float32)
    large = exact + (np.log(nf / np.float32(exact)) / np.float32(math.log(MAX_DISTANCE / exact))
                     * np.float32(NUM_BUCKETS - exact)).astype(np.int32)
    return np.where(n < exact, n, np.minimum(large, NUM_BUCKETS - 1)).astype(np.int32)


def _row_tile(rows):
    return 256 if rows % 256 == 0 else rows


def _tok_call(body, rows, row_ins, const_ins, outs):
    tm = _row_tile(rows)
    in_specs = [pl.BlockSpec((tm, a.shape[1]), lambda i: (i, 0)) for a in row_ins]
    in_specs += [_const_spec(a.shape) for a in const_ins]
    out_specs = [pl.BlockSpec((tm, w), lambda i: (i, 0)) for w, _ in outs]
    out_shape = [jax.ShapeDtypeStruct((rows, w), dt) for w, dt in outs]
    return pl.pallas_call(
        body, grid=(rows // tm,), in_specs=in_specs, out_specs=out_specs, out_shape=out_shape,
        name=body.__name__.strip("_").removesuffix("_body"), compiler_params=_cparams(("parallel",)))(*row_ins, *const_ins)


A_CQ, A_CKV, A_G, A_Z, A_XBC, A_MISC, A_END = 0, 384, 640, 1152, 2176, 3712, 3840


def _a_front_body(x_ref, ccq_ref, ssq_ref, cck_ref, ssk_ref, w_ref, qn_ref, kvn_ref, wuq_ref, wuk_ref, dtb_ref,
                  qlat_ref, qrope_ref, lat_ref, latb_ref, kr_ref, krb_ref, g_ref, z_ref, xbc_ref, dt_ref):
    xb = x_ref[...].astype(BF16)

    def proj(lo, hi):
        return _dot(xb, w_ref[:, lo:hi])

    cqn = _rms(proj(A_CQ, A_CKV), qn_ref[...])
    qall = _dot(cqn.astype(BF16), wuq_ref[...])
    qrope = (qall[:, 512:768] * ccq_ref[...] + qall[:, 768:1024] * ssq_ref[...]) * MLA_SCALE
    qrope_ref[...] = qrope.astype(BF16)
    qlat = _dot(qall[:, :512].astype(BF16), wuk_ref[...]) * MLA_SCALE
    qlat_ref[...] = qlat.astype(BF16)
    lat = _rms(proj(A_CKV, A_G), kvn_ref[...])
    lat_ref[...] = lat
    latb_ref[...] = lat.astype(BF16)
    g_ref[...] = proj(A_G, A_Z)
    z_ref[...] = proj(A_Z, A_XBC)
    xbc_ref[...] = proj(A_XBC, A_MISC)
    misc = proj(A_MISC, A_END)
    kr = misc[:, 0:32] * cck_ref[...] + misc[:, 32:64] * ssk_ref[...]
    kr_ref[...] = kr
    krb_ref[...] = kr.astype(BF16)
    v = misc[:, 64:80] + dtb_ref[...]
    dt_ref[...] = jnp.maximum(v, 0.0) + jnp.log1p(jnp.exp(-jnp.abs(v)))


def _rot_cols(w):
    h = ROPE // 2
    return jnp.concatenate([-w[..., h:], w[..., :h]], axis=-1)


def _prep_a(w_in, q_norm, w_uq, kv_norm, w_uk, w_uv, dt_bias, w_out):
    cq, ckv, kr, g, z, xbc, dtw = jnp.split(w_in, np.cumsum([Q_RANK, KV_RANK, ROPE, MLA_WIDTH, D_INNER, CONV_DIM])[:].tolist(),
                                            axis=1)
    pad = jnp.zeros((w_in.shape[0], A_END - A_MISC - 2 * ROPE - SSM_HEADS), w_in.dtype)
    w1 = jnp.concatenate([cq, ckv, g, z, xbc, kr, _rot_cols(kr), dtw, pad], axis=1).astype(BF16)
    wq = w_uq.reshape(Q_RANK, MLA_HEADS, NOPE + ROPE)
    wq_n = wq[:, :, :NOPE].reshape(Q_RANK, MLA_HEADS * NOPE)
    wq_r = wq[:, :, NOPE:]
    wuq = jnp.concatenate([wq_n, wq_r.reshape(Q_RANK, -1), _rot_cols(wq_r).reshape(Q_RANK, -1)], axis=1).astype(BF16)
    eye = jnp.eye(MLA_HEADS, dtype=w_uk.dtype)
    wuk = (jnp.transpose(w_uk, (1, 2, 0))[:, :, None, :] * eye[:, None, :, None]).reshape(
        MLA_HEADS * NOPE, MLA_HEADS * KV_RANK).astype(BF16)
    wuv = (jnp.transpose(w_uv, (1, 0, 2))[:, :, None, :] * eye[:, None, :, None]).reshape(
        MLA_HEADS * KV_RANK, MLA_WIDTH).astype(BF16)
    return dict(w1=w1, qn=q_norm.reshape(1, -1), kvn=kv_norm.reshape(1, -1), wuq=wuq, wuk=wuk, wuv=wuv,
                dtb=dt_bias.reshape(1, -1), wout=w_out.astype(BF16))


def _rope_tables(pos):
    half = ROPE // 2
    inv = ROPE_BASE ** (-jnp.arange(half, dtype=F32) / half)
    ang = pos.astype(F32)[:, None] * inv[None, :]
    c, s = jnp.cos(ang), jnp.sin(ang)
    cck, ssk = jnp.concatenate([c, c], 1), jnp.concatenate([s, s], 1)
    return jnp.tile(cck, (1, MLA_HEADS)), jnp.tile(ssk, (1, MLA_HEADS)), cck, ssk


def _a_front(x, pos, pa):
    rows = x.shape[0]
    ccq, ssq, cck, ssk = _rope_tables(pos)
    outs = [(MLA_HEADS * KV_RANK, BF16), (MLA_HEADS * ROPE, BF16), (KV_RANK, F32), (KV_RANK, BF16), (ROPE, F32),
            (ROPE, BF16), (MLA_WIDTH, F32), (D_INNER, F32), (CONV_DIM, F32), (SSM_HEADS, F32)]
    return _tok_call(_a_front_body, rows, [x, ccq, ssq, cck, ssk],
                     [pa["w1"], pa["qn"], pa["kvn"], pa["wuq"], pa["wuk"], pa["dtb"]], outs)


def _a_back_body(x_ref, ol_ref, g_ref, y_ref, z_ref, wuv_ref, sn_ref, wo_ref, lg_ref, lb_ref, o_ref):
    o_mla = _dot(ol_ref[...], wuv_ref[...])
    a = o_mla * _silu(g_ref[...])
    yn = _rms(y_ref[...] * _silu(z_ref[...]), sn_ref[...])
    out = _dot(a.astype(BF16), wo_ref[:MLA_WIDTH, :]) + _dot(yn.astype(BF16), wo_ref[MLA_WIDTH:, :])
    o_ref[...] = _deepnorm_ln(x_ref[...], out, lg_ref[...], lb_ref[...])


def _a_back(x, o_lat, g, y, z, pa, ssm_norm, ln_g, ln_b):
    (out,) = _tok_call(_a_back_body, x.shape[0], [x, o_lat, g, y, z],
                       [pa["wuv"], ssm_norm.reshape(1, -1), pa["wout"], ln_g.reshape(1, -1), ln_b.reshape(1, -1)],
                       [(D_MODEL, F32)])
    return out


C_Q, C_KV, C_OG, C_END = 0, 1024, 1792, 2816


def _c_front_body(x_ref, w_ref, q_ref, kc_ref, vc_ref, ks_ref, vs_ref, kw_ref, vw_ref, ksb_ref, vsb_ref, kwb_ref,
                  vwb_ref, og_ref):
    xb = x_ref[...].astype(BF16)
    q_ref[...] = _dot(xb, w_ref[:, C_Q:C_KV]) * NSA_SCALE
    kv = _dot(xb, w_ref[:, C_KV:C_OG])
    for j, r in enumerate((kc_ref, vc_ref, ks_ref, vs_ref, kw_ref, vw_ref)):
        r[...] = kv[:, 128 * j:128 * (j + 1)]
    for j, r in enumerate((ksb_ref, vsb_ref, kwb_ref, vwb_ref)):
        r[...] = kv[:, 128 * (j + 2):128 * (j + 3)].astype(BF16)
    og_ref[...] = _dot(xb, w_ref[:, C_OG:C_END])


def _prep_c(w_in, w_out):
    n_kv = 6 * NSA_GROUPS * NSA_D
    w_main = jnp.concatenate([w_in[:, :1024 + n_kv], w_in[:, 1024 + n_kv + 3 * NSA_HEADS:]], axis=1).astype(BF16)
    w_g = w_in[:, 1024 + n_kv:1024 + n_kv + 3 * NSA_HEADS]
    w_grep = jnp.repeat(w_g, NSA_D, axis=1).astype(BF16)
    return dict(w_main=w_main, w_grep=w_grep, wout=w_out.astype(BF16))


def _c_front(x, pc):
    outs = [(1024, F32)] + [(128, F32)] * 6 + [(128, BF16)] * 4 + [(1024, F32)]
    return _tok_call(_c_front_body, x.shape[0], [x], [pc["w_main"]], outs)


def _c_back_body(x_ref, oc_ref, os_ref, ow_ref, og_ref, wg_ref, wo_ref, lg_ref, lb_ref, o_ref):
    x = x_ref[...]
    gates = _sigmoid(_dot(x.astype(BF16), wg_ref[...]))
    o = gates[:, :1024] * oc_ref[...] + gates[:, 1024:2048] * os_ref[...] + gates[:, 2048:] * ow_ref[...]
    out = _dot((o * _silu(og_ref[...])).astype(BF16), wo_ref[...])
    o_ref[...] = _deepnorm_ln(x, out, lg_ref[...], lb_ref[...])


def _c_back(x, o_c, o_s, o_w, og, pc, ln_g, ln_b):
    (out,) = _tok_call(_c_back_body, x.shape[0], [x, o_c, o_s, o_w, og],
                       [pc["w_grep"], pc["wout"], ln_g.reshape(1, -1), ln_b.reshape(1, -1)], [(D_MODEL, F32)])
    return out


def _online_update(s, mask, v_fn, m_ref, l_ref, acc_ref):
    if mask is not None:
        s = jnp.where(mask, s, NEG)
    m_prev = m_ref[...]
    m_new = jnp.maximum(m_prev, jnp.max(s, axis=1, keepdims=True))
    p = jnp.exp2(s - m_new)
    if mask is not None:
        p = jnp.where(mask, p, 0.0)
    corr = jnp.exp2(m_prev - m_new)
    l_ref[...] = l_ref[...] * corr + jnp.sum(p, axis=1, keepdims=True)
    acc_ref[...] = acc_ref[...] * corr + v_fn(p.astype(BF16))
    m_ref[...] = m_new


LANES = 128


def _lane_tile(x, n):
    return x if n == 1 else jnp.concatenate([x] * n, axis=1)


def _flash_step(s, v_fn, m_ref, l_ref, acc_ref):
    m_prev = m_ref[...]
    m_new = jnp.maximum(m_prev, jnp.max(s, axis=1, keepdims=True))
    p = jnp.exp2(s - _lane_tile(m_new, s.shape[1] // LANES))
    corr = jnp.exp2(m_prev - m_new)
    l_ref[...] = l_ref[...] * corr + jnp.sum(p, axis=1, keepdims=True)
    dv = acc_ref.shape[-1]
    acc_ref[...] = acc_ref[...] * _lane_tile(corr, dv // LANES) + v_fn(p.astype(BF16))
    m_ref[...] = m_new


def _flash_out(l_ref, acc_ref):
    dv = acc_ref.shape[-1]
    inv = 1.0 / l_ref[...]
    return acc_ref[...] * _lane_tile(inv, dv // LANES)


NEW_PAD = 16


def _pad_rows16(x):
    return jnp.concatenate([x, jnp.zeros((NEW_PAD - x.shape[0], x.shape[1]), x.dtype)], axis=0)


def _flash_init(m_ref, l_ref, acc_ref):
    m_ref[...] = jnp.full(m_ref.shape, NEG, F32)
    l_ref[...] = jnp.zeros(l_ref.shape, F32)
    acc_ref[...] = jnp.zeros(acc_ref.shape, F32)


def _flash_result(l_ref, acc_ref):
    l = l_ref[...]
    return acc_ref[...] / jnp.where(l > 0.0, l, 1.0)


def _causal_schedule(n_q, tq, tk):
    qi, ki = [], []
    for i in range(n_q):
        last = (i * tq + tq - 1) // tk
        for k in range(last + 1):
            qi.append(i)
            ki.append(k)
    return jnp.asarray(np.array(qi, np.int32)), jnp.asarray(np.array(ki, np.int32))


def _key_tile(seq, want):
    tk = want
    while seq % tk:
        tk //= 2
    return max(tk, TQ)


def _mla_prompt_body(qi_ref, ki_ref, ql_ref, qr_ref, lat_ref, kr_ref, o_ref, m_ref, l_ref, acc_ref, *, tk):
    step = pl.program_id(0)
    qi, ki = qi_ref[step], ki_ref[step]

    last = (qi * TQ + TQ - 1) // tk

    @pl.when(ki == 0)
    def _():
        _flash_init(m_ref, l_ref, acc_ref)

    def run(diagonal):
        ts = tk // 2 if tk >= 2 * LANES else tk
        for h in range(tk // ts):
            lat = lat_ref[h * ts:(h + 1) * ts, :]
            s = _dot_nt(ql_ref[...], lat) + _dot_nt(qr_ref[...], kr_ref[h * ts:(h + 1) * ts, :])
            if diagonal:
                rows = s.shape[0]
                tok = qi * TQ - ki * tk - h * ts + (lax.broadcasted_iota(jnp.int32, (rows, ts), 0) >> 3)
                s = jnp.where(lax.broadcasted_iota(jnp.int32, (rows, ts), 1) <= tok, s, NEG)
            _flash_step(s, lambda p, lat=lat: _dot(p, lat), m_ref, l_ref, acc_ref)

    @pl.when(ki < last)
    def _():
        run(False)

    @pl.when(ki == last)
    def _():
        run(True)
        o_ref[...] = _flash_out(l_ref, acc_ref).astype(o_ref.dtype)


def _mla_prompt(ql, qr, latb, krb):
    seq = latb.shape[0]
    tk = _key_tile(seq, 1024)
    rows = TQ * MLA_HEADS
    qi, ki = _causal_schedule(seq // TQ, TQ, tk)
    grid_spec = pltpu.PrefetchScalarGridSpec(
        num_scalar_prefetch=2, grid=(qi.shape[0],),
        in_specs=[pl.BlockSpec((rows, KV_RANK), lambda s, q, k: (q[s], 0)),
                  pl.BlockSpec((rows, ROPE), lambda s, q, k: (q[s], 0)),
                  pl.BlockSpec((tk, KV_RANK), lambda s, q, k: (k[s], 0)),
                  pl.BlockSpec((tk, ROPE), lambda s, q, k: (k[s], 0))],
        out_specs=pl.BlockSpec((rows, KV_RANK), lambda s, q, k: (q[s], 0)),
        scratch_shapes=[pltpu.VMEM((rows, LANES), F32), pltpu.VMEM((rows, LANES), F32),
                        pltpu.VMEM((rows, KV_RANK), F32)])
    return pl.pallas_call(
        functools.partial(_mla_prompt_body, tk=tk), grid_spec=grid_spec, name="mla_prompt",
        out_shape=jax.ShapeDtypeStruct((seq * MLA_HEADS, KV_RANK), BF16),
        compiler_params=_cparams(("arbitrary",)))(qi, ki, ql, qr, latb, krb)


def _pages_per_step(n_pages):
    for pp in (32, 16, 8, 4, 2):
        if n_pages % pp == 0:
            return pp
    return 1


def _page_copy(pool_ref, buf_ref, sem_ref, li, page, slot, j):
    return pltpu.make_async_copy(pool_ref.at[li, page], buf_ref.at[slot, j], sem_ref.at[slot])


def _paged_pipeline(pt_ref, pools, bufs, sems, li, pp):
    b, c = pl.program_id(0), pl.program_id(1)
    nb, nc = pl.num_programs(0), pl.num_programs(1)
    step = b * nc + c
    slot = step % 2

    def fetch(b_, c_, slot_):
        for j in range(pp):
            page = pt_ref[b_, c_ * pp + j]
            for pool, buf, sem in zip(pools, bufs, sems):
                _page_copy(pool, buf, sem, li, page, slot_, j).start()

    @pl.when(step == 0)
    def _():
        fetch(0, 0, 0)

    @pl.when(step + 1 < nb * nc)
    def _():
        wrap = c + 1 == nc
        fetch(jnp.where(wrap, b + 1, b), jnp.where(wrap, 0, c + 1), 1 - slot)

    for pool, buf, sem in zip(pools, bufs, sems):
        pltpu.make_async_copy(pool.at[li, pl.ds(0, pp)], buf.at[slot], sem.at[slot]).wait()
    return slot


def _mla_sample_body(pt_ref, ql_ref, qr_ref, latn_ref, krn_ref, lat_hbm, kr_hbm, o_ref,
                     latbuf, krbuf, sems, m_ref, l_ref, acc_ref, *, pp, li):
    slot = _paged_pipeline(pt_ref, (lat_hbm, kr_hbm), (latbuf, krbuf), (sems.at[0], sems.at[1]), li, pp)
    pc = pl.program_id(1)

    @pl.when(pc == 0)
    def _():
        _flash_init(m_ref, l_ref, acc_ref)

    ql, qr = ql_ref[...], qr_ref[...]
    lats = [latbuf[slot, j].astype(BF16) for j in range(pp)]
    s = jnp.concatenate([_dot_nt(ql, lats[j]) + _dot(qr, krbuf[slot, j].astype(BF16)) for j in range(pp)], axis=1)

    def pv(p):
        acc = _dot(p[:, :PAGE], lats[0])
        for j in range(1, pp):
            acc = acc + _dot(p[:, j * PAGE:(j + 1) * PAGE], lats[j])
        return acc

    _online_update(s, None, pv, m_ref, l_ref, acc_ref)

    @pl.when(pc == pl.num_programs(1) - 1)
    def _():
        latn = _pad_rows16(latn_ref[...]).astype(BF16)
        sn = _dot_nt(ql, latn) + _dot_nt(qr, _pad_rows16(krn_ref[...]).astype(BF16))
        rows = sn.shape[0]
        tok = lax.broadcasted_iota(jnp.int32, (rows, NEW_PAD), 0) >> 3
        key = lax.broadcasted_iota(jnp.int32, (rows, NEW_PAD), 1)
        _online_update(sn, key <= tok, lambda p: _dot(p, latn), m_ref, l_ref, acc_ref)
        o_ref[...] = _flash_result(l_ref, acc_ref).astype(o_ref.dtype)


def _mla_sample(ql, qr, lat_new, kr_new, pool_lat, pool_kr, li, page_table, n_new):
    db, n_pages = page_table.shape
    pp = _pages_per_step(n_pages)
    rows = n_new * MLA_HEADS
    row_spec = lambda w: pl.BlockSpec((rows, w), lambda b, c, pt: (b, 0))
    new_spec = lambda w: pl.BlockSpec((n_new, w), lambda b, c, pt: (b, 0))
    hbm_spec = pl.BlockSpec(memory_space=pl.ANY)
    pool_kr = jnp.transpose(pool_kr, (0, 1, 3, 2))
    grid_spec = pltpu.PrefetchScalarGridSpec(
        num_scalar_prefetch=1, grid=(db, n_pages // pp),
        in_specs=[row_spec(KV_RANK), row_spec(ROPE), new_spec(KV_RANK), new_spec(ROPE), hbm_spec, hbm_spec],
        out_specs=row_spec(KV_RANK),
        scratch_shapes=[pltpu.VMEM((2, pp, PAGE, KV_RANK), F32), pltpu.VMEM((2, pp, ROPE, PAGE), F32),
                        pltpu.SemaphoreType.DMA((2, 2)),
                        pltpu.VMEM((rows, 1), F32), pltpu.VMEM((rows, 1), F32), pltpu.VMEM((rows, KV_RANK), F32)])
    return pl.pallas_call(
        functools.partial(_mla_sample_body, pp=pp, li=li), grid_spec=grid_spec, name="mla_sample",
        out_shape=jax.ShapeDtypeStruct((db * rows, KV_RANK), BF16),
        compiler_params=_cparams(("arbitrary", "arbitrary")))(
            page_table, ql, qr, lat_new, kr_new, pool_lat, pool_kr)


def _ssd_body(xbc_ref, dt_ref, dtt_ref, conv0_ref, h0_ref, cw_ref, cb_ref, alr_ref, alc_ref, dsk_ref,
              y_ref, convn_ref, hn_ref, xpad_ref, h_ref, *, q):
    c = pl.program_id(1)
    keep = CONV_W - 1

    @pl.when(c == 0)
    def _():
        h_ref[...] = h0_ref[...]
        xpad_ref[8 - keep:8, :] = conv0_ref[...]

    xpad_ref[8:8 + q, :] = xbc_ref[...]
    conv = cb_ref[...] + xpad_ref[8 - keep:8 - keep + q, :] * cw_ref[0:1, :]
    for w in range(1, CONV_W):
        conv = conv + xpad_ref[8 - keep + w:8 - keep + w + q, :] * cw_ref[w:w + 1, :]
    tail = xpad_ref[8 + q - keep:8 + q, :]
    convn_ref[...] = tail
    xpad_ref[8 - keep:8, :] = tail

    u = _silu(conv)
    xs = u[:, :D_INNER]
    gn = SSM_GROUPS * D_STATE
    bm, cm = u[:, D_INNER:D_INNER + gn], u[:, D_INNER + gn:]

    dt = dt_ref[...]
    dtt = dtt_ref[...]
    a_row = -jnp.exp(alr_ref[...])
    a_col = -jnp.exp(alc_ref[...])
    ii = lax.broadcasted_iota(jnp.int32, (q, q), 0)
    jj = lax.broadcasted_iota(jnp.int32, (q, q), 1)
    tril = ii >= jj
    cs = jnp.dot(tril.astype(F32), dt * a_row, precision=lax.Precision.HIGHEST, preferred_element_type=F32)
    cst = jnp.dot(dtt * a_col, (ii <= jj).astype(F32), precision=lax.Precision.HIGHEST, preferred_element_type=F32)
    rpg = SSM_HEADS // SSM_GROUPS
    for g in range(SSM_GROUPS):
        bg = bm[:, g * D_STATE:(g + 1) * D_STATE]
        cg = cm[:, g * D_STATE:(g + 1) * D_STATE].astype(BF16)
        cb = _dot_nt(cg, bg.astype(BF16))
        for r in range(rpg):
            h = g * rpg + r
            cs_col, cs_row = cs[:, h:h + 1], cst[h:h + 1, :]
            dt_col, dt_row = dt[:, h:h + 1], dtt[h:h + 1, :]
            cs_last = cst[h:h + 1, q - 1:q]
            decay = jnp.where(tril, jnp.exp(jnp.minimum(cs_col - cs_row, 0.0)), 0.0)
            lmat = (cb * decay * dt_row).astype(BF16)
            xs_h = xs[:, h * SSM_HEADDIM:(h + 1) * SSM_HEADDIM]
            xs_hb = xs_h.astype(BF16)
            y_diag = _dot(lmat, xs_hb)
            h_prev = h_ref[h]
            y_off = _dot_nt(cg, h_prev.astype(BF16)) * jnp.exp(cs_col)
            y_ref[:, h * SSM_HEADDIM:(h + 1) * SSM_HEADDIM] = y_diag + y_off + xs_h * dsk_ref[0:1, h:h + 1]
            bw = (bg * (jnp.exp(cs_last - cs_col) * dt_col)).astype(BF16)
            h_ref[h] = h_prev * jnp.exp(cs_last) + _dot_tn(xs_hb, bw)
    hn_ref[...] = h_ref[...]


def _ssd(xbc, dt, conv0, h0, conv_w, conv_b, a_log, d_skip, batch):
    rows = xbc.shape[0]
    length = rows // batch
    q = SSD_CHUNK if length % SSD_CHUNK == 0 else length
    nc = length // q
    dtt = jnp.transpose(dt.reshape(batch * nc, q, SSM_HEADS), (0, 2, 1))
    keep = CONV_W - 1
    in_specs = [pl.BlockSpec((q, CONV_DIM), lambda b, c: (b * nc + c, 0)),
                pl.BlockSpec((q, SSM_HEADS), lambda b, c: (b * nc + c, 0)),
                pl.BlockSpec((None, SSM_HEADS, q), lambda b, c: (b * nc + c, 0, 0)),
                pl.BlockSpec((None, keep, CONV_DIM), lambda b, c: (b, 0, 0)),
                pl.BlockSpec((None, SSM_HEADS, SSM_HEADDIM, D_STATE), lambda b, c: (b, 0, 0, 0)),
                _const_spec((CONV_W, CONV_DIM)), _const_spec((1, CONV_DIM)), _const_spec((1, SSM_HEADS)),
                _const_spec((SSM_HEADS, 1)), _const_spec((1, SSM_HEADS))]
    out_specs = [pl.BlockSpec((q, D_INNER), lambda b, c: (b * nc + c, 0)),
                 pl.BlockSpec((None, keep, CONV_DIM), lambda b, c: (b, 0, 0)),
                 pl.BlockSpec((None, SSM_HEADS, SSM_HEADDIM, D_STATE), lambda b, c: (b, 0, 0, 0))]
    out_shape = [jax.ShapeDtypeStruct((rows, D_INNER), F32), jax.ShapeDtypeStruct((batch, keep, CONV_DIM), F32),
                 jax.ShapeDtypeStruct((batch, SSM_HEADS, SSM_HEADDIM, D_STATE), F32)]
    return pl.pallas_call(
        functools.partial(_ssd_body, q=q), grid=(batch, nc), in_specs=in_specs, out_specs=out_specs,
        out_shape=out_shape, name="ssd",
        scratch_shapes=[pltpu.VMEM((8 + q, CONV_DIM), F32), pltpu.VMEM((SSM_HEADS, SSM_HEADDIM, D_STATE), F32)],
        compiler_params=_cparams(("parallel", "arbitrary")))(
            xbc, dt, dtt, conv0, h0, conv_w, conv_b.reshape(1, -1), a_log.reshape(1, -1), a_log.reshape(-1, 1),
            d_skip.reshape(1, -1))


CHUNK_W = CMP_STRIDE * NSA_GROUPS * NSA_D
KC_PAD = 16
NEAR = 32


def _prep_cmp(pe, w1, w2):
    eye = jnp.eye(NSA_GROUPS, dtype=w1.dtype)

    def big(w):
        return (w[:, None, :, None, :] * eye[None, :, None, :, None]).reshape(CHUNK_W, NSA_GROUPS * NSA_D)

    w_big = jnp.concatenate([big(w1[:CMP_STRIDE]), big(w1[CMP_STRIDE:])], axis=1).astype(BF16)
    peb = jnp.einsum('ld,ldh->h', pe, w1)
    peb2 = jnp.tile(peb, NSA_GROUPS).reshape(1, -1)
    w2_bd = (w2[None, :, None, :] * eye[:, None, :, None]).reshape(NSA_GROUPS * NSA_D, NSA_GROUPS * NSA_D).astype(BF16)
    return w_big, peb2, w2_bd


def _mm_body(x_ref, w_ref, o_ref):
    o_ref[...] = _dot(x_ref[...].astype(BF16), w_ref[...])


def _cmp_u_prompt(rows_kv, w_big):
    chunks = rows_kv.reshape(-1, CHUNK_W)
    (u,) = _tok_call(_mm_body, chunks.shape[0], [chunks], [w_big], [(w_big.shape[1], F32)])
    return u


def _feature_major(pool):
    return jnp.transpose(pool, (0, 1, 3, 4, 2))


def _cmp_u_paged_body(pt_ref, wk_ref, wv_ref, k_hbm, v_hbm, uk_ref, uv_ref, kbuf, vbuf, sems, tok_ref, *, pp, li):
    slot = _paged_pipeline(pt_ref, (k_hbm, v_hbm), (kbuf, vbuf), (sems.at[0], sems.at[1]), li, pp)
    feat = NSA_GROUPS * NSA_D
    cpp = PAGE // CMP_STRIDE
    for buf, w_ref, o_ref in ((kbuf, wk_ref, uk_ref), (vbuf, wv_ref, uv_ref)):
        for j in range(pp):
            tok_ref[j * PAGE:(j + 1) * PAGE, :] = buf[slot, j].reshape(feat, PAGE).T
        acc = None
        for l in range(CMP_STRIDE):
            x = tok_ref[pl.ds(l, pp * cpp, stride=CMP_STRIDE), :].astype(BF16)
            part = _dot(x, w_ref[l * feat:(l + 1) * feat, :])
            acc = part if acc is None else acc + part
        o_ref[...] = acc


def _cmp_u_paged(pool_k, pool_v, li, page_table, wk_big, wv_big):
    db, n_pages = page_table.shape
    pp = _pages_per_step(n_pages)
    cpp = PAGE // CMP_STRIDE
    width = wk_big.shape[1]
    hbm_spec = pl.BlockSpec(memory_space=pl.ANY)
    w_spec = pl.BlockSpec(wk_big.shape, lambda b, c, pt: (0, 0))
    u_spec = pl.BlockSpec((None, pp * cpp, width), lambda b, c, pt: (b, c, 0))
    page_buf = pltpu.VMEM((2, pp, NSA_GROUPS, NSA_D, PAGE), F32)
    grid_spec = pltpu.PrefetchScalarGridSpec(
        num_scalar_prefetch=1, grid=(db, n_pages // pp),
        in_specs=[w_spec, w_spec, hbm_spec, hbm_spec], out_specs=[u_spec, u_spec],
        scratch_shapes=[page_buf, page_buf, pltpu.SemaphoreType.DMA((2, 2)),
                        pltpu.VMEM((pp * PAGE, NSA_GROUPS * NSA_D), F32)])
    u_shape = jax.ShapeDtypeStruct((db, n_pages * cpp, width), F32)
    return pl.pallas_call(
        functools.partial(_cmp_u_paged_body, pp=pp, li=li), grid_spec=grid_spec, name="cmp_u_paged",
        out_shape=[u_shape, u_shape],
        compiler_params=_cparams(("arbitrary", "arbitrary")))(
            page_table, wk_big, wv_big, _feature_major(pool_k), _feature_major(pool_v))


def _cmp_finish_body(uk_ref, uv_ref, pk_ref, pv_ref, wk_ref, wv_ref, kc_ref, vc_ref, *, n):
    half = NSA_GROUPS * NSA_D

    def fin(u_ref, p_ref, w_ref, o_ref):
        u = u_ref[...]
        hid = u[:, :half] + pltpu.roll(u[:, half:], n - 1, 0) + p_ref[...]
        o_ref[...] = jnp.zeros(o_ref.shape, F32)
        o_ref[KC_PAD:KC_PAD + n, :] = _dot(_silu(hid).astype(BF16), w_ref[...])

    fin(uk_ref, pk_ref, wk_ref, kc_ref)
    fin(uv_ref, pv_ref, wv_ref, vc_ref)


def _cmp_finish(uk, uv, pk, pv, batch):
    n = uk.shape[0] // batch
    half = NSA_GROUPS * NSA_D
    n_out = KC_PAD + n + NEAR
    u_spec = pl.BlockSpec((n, 2 * half), lambda b: (b, 0))
    o_spec = pl.BlockSpec((None, n_out, half), lambda b: (b, 0, 0))
    consts = [pk[1], pv[1], pk[2], pv[2]]
    return pl.pallas_call(
        functools.partial(_cmp_finish_body, n=n), grid=(batch,), name="cmp_finish",
        in_specs=[u_spec, u_spec] + [_const_spec(a.shape) for a in consts], out_specs=[o_spec, o_spec],
        out_shape=[jax.ShapeDtypeStruct((batch, n_out, half), F32)] * 2,
        compiler_params=_cparams(("parallel",)))(uk, uv, *consts)


def _stack_heads(q, g):
    t = q.shape[0]
    lane = lax.broadcasted_iota(jnp.int32, (t, 128), 1)
    lo = lane < NSA_D
    parts = []
    for m in range(g * 4, g * 4 + 4):
        x = q[:, 128 * m:128 * (m + 1)]
        xr = pltpu.roll(x, NSA_D, 1)
        if g == 0:
            parts += [jnp.where(lo, x, 0.0), jnp.where(lo, xr, 0.0)]
        else:
            parts += [jnp.where(lo, 0.0, xr), jnp.where(lo, 0.0, x)]
    return jnp.concatenate(parts, axis=0)


def _unstack_heads(o, g, t):
    lane = lax.broadcasted_iota(jnp.int32, (t, 128), 1)
    lo = lane < NSA_D
    blocks = []
    for m in range(4):
        a, b = o[(2 * m) * t:(2 * m + 1) * t], o[(2 * m + 1) * t:(2 * m + 2) * t]
        if g == 0:
            blocks.append(jnp.where(lo, a, pltpu.roll(b, NSA_D, 1)))
        else:
            blocks.append(jnp.where(lo, pltpu.roll(a, NSA_D, 1), b))
    return blocks


def _bias_rows(rel_bias, dist):
    t, k = dist.shape
    b = rel_bias[jnp.asarray(_t5_bucket_np(dist))] * LOG2E
    return jnp.transpose(b, (2, 0, 1)).reshape(NSA_GROUPS, NSA_HPG * t, k).astype(F32)


def _far_bias(rel_bias, t):
    col = jnp.repeat(rel_bias[NUM_BUCKETS - 1] * LOG2E, t)
    return col.reshape(NSA_GROUPS, NSA_HPG * t, 1).astype(F32)


def _topk_mask(score, lane_f, n_rounds):
    sel = jnp.zeros(score.shape, F32)
    for _ in range(n_rounds):
        mx = jnp.max(score, axis=1, keepdims=True)
        idx = jnp.min(jnp.where(score == mx, lane_f, 1e9), axis=1, keepdims=True)
        pick = lane_f == idx
        sel = jnp.where(pick, 1.0, sel)
        score = jnp.where(pick, -2.0, score)
    return sel


def _select_scores(imp, tpos, n_slc):
    lane = lax.broadcasted_iota(jnp.int32, imp.shape, 1)
    cur = tpos >> 6
    forced = (lane == 0) | (lane == cur) | (lane == cur - 1)
    score = jnp.where(lane <= cur, jnp.where(forced, 1e30, imp), -1.0)
    score = jnp.where(lane < n_slc, score, -10.0)
    return score, lane.astype(F32)


def _cmp_prompt_body(q_ref, kc_ref, vc_ref, ov_ref, bn_ref, bf_ref, oc_ref, sel_ref, *, n_cmp, n_slc):
    i = pl.program_id(0)
    start = pl.multiple_of(i * (TQ // CMP_STRIDE), 8)
    n_all = kc_ref.shape[0]
    widths = [min(n_all, KC_PAD + CMP_FAR_STEP * (c + 1)) for c in range(-(-(n_all - KC_PAD) // CMP_FAR_STEP))]
    cls = jnp.minimum(jnp.maximum(start - KC_PAD - 1, 0) // CMP_FAR_STEP, len(widths) - 1)
    for c, width in enumerate(widths):
        @pl.when(cls == c)
        def _(width=width):
            _cmp_prompt_tile(q_ref, kc_ref, vc_ref, ov_ref, bn_ref, bf_ref, oc_ref, sel_ref, i, start, width,
                             n_cmp, n_slc)


CMP_FAR_STEP = 256


def _cmp_prompt_tile(q_ref, kc_ref, vc_ref, ov_ref, bn_ref, bf_ref, oc_ref, sel_ref, i, start, n_rows, n_cmp, n_slc):
    q = q_ref[...]
    kc_all, vc_all = kc_ref[0:n_rows, :].astype(BF16), vc_ref[0:n_rows, :].astype(BF16)
    kc_near = kc_ref[pl.ds(start, NEAR), :].astype(BF16)
    vc_near = vc_ref[pl.ds(start, NEAR), :].astype(BF16)
    ov_all = ov_ref[0:n_rows, :].astype(BF16)
    ov_near = ov_ref[pl.ds(start, NEAR), :].astype(BF16)
    rows = NSA_HPG * TQ
    colf = lax.broadcasted_iota(jnp.int32, (rows, n_rows), 1)
    mask_f = (colf >= KC_PAD) & (colf < start)
    t_n = lax.broadcasted_iota(jnp.int32, (rows, NEAR), 0) & (TQ - 1)
    jj = lax.broadcasted_iota(jnp.int32, (rows, NEAR), 1)
    jabs = start - KC_PAD + jj
    mask_n = (t_n - CMP_STRIDE * (jj - KC_PAD) - (CMP_LEN - 1) >= 0) & (jabs >= 0) & (jabs < n_cmp)
    tpos = i * TQ + lax.broadcasted_iota(jnp.int32, (TQ, 1), 0)
    imps = []
    for g in range(NSA_GROUPS):
        qg = _stack_heads(q, g).astype(BF16)
        s_f = jnp.where(mask_f, _dot_nt(qg, kc_all) + bf_ref[g], NEG)
        s_n = jnp.where(mask_n, _dot_nt(qg, kc_near) + bn_ref[g], NEG)
        m = jnp.maximum(jnp.max(s_f, axis=1, keepdims=True), jnp.max(s_n, axis=1, keepdims=True))
        p_f = jnp.where(mask_f, jnp.exp2(s_f - m), 0.0)
        p_n = jnp.where(mask_n, jnp.exp2(s_n - m), 0.0)
        l = jnp.sum(p_f, axis=1, keepdims=True) + jnp.sum(p_n, axis=1, keepdims=True)
        inv = 1.0 / jnp.where(l > 0.0, l, 1.0)
        p_f, p_n = p_f * inv, p_n * inv
        o = _dot(p_f.astype(BF16), vc_all) + _dot(p_n.astype(BF16), vc_near)
        for m4, blk in enumerate(_unstack_heads(o, g, TQ)):
            col = (g * 4 + m4) * 128
            oc_ref[:, col:col + 128] = blk
        ps_f, ps_n = p_f[:TQ], p_n[:TQ]
        for r in range(1, NSA_HPG):
            ps_f = ps_f + p_f[r * TQ:(r + 1) * TQ]
            ps_n = ps_n + p_n[r * TQ:(r + 1) * TQ]
        imps.append(_dot(ps_f.astype(BF16), ov_all) + _dot(ps_n.astype(BF16), ov_near))
    score, lane_f = _select_scores(jnp.concatenate(imps, axis=0), jnp.concatenate([tpos] * NSA_GROUPS, axis=0), n_slc)
    picked = _topk_mask(score, lane_f, min(N_SEL, n_slc))
    for g in range(NSA_GROUPS):
        sel_ref[g] = jnp.where(picked[g * TQ:(g + 1) * TQ] > 0.5, 0.0, NEG).astype(sel_ref.dtype)


def _overlap_np(n_rows, n_cmp, n_slc, width):
    ov = np.zeros((n_rows, width), np.float32)
    cs = np.arange(n_cmp)[:, None] * CMP_STRIDE
    bs = np.arange(n_slc)[None, :] * SLC_BLOCK
    ov[KC_PAD:KC_PAD + n_cmp, :n_slc] = np.maximum(
        np.minimum(cs + CMP_LEN, bs + SLC_BLOCK) - np.maximum(cs, bs), 0)
    return ov


def _cmp_prompt(q, kc, vc, rel_bias, seq):
    n_rows = kc.shape[0]
    n_cmp = (seq - CMP_LEN) // CMP_STRIDE + 1
    n_slc = -(-seq // SLC_BLOCK)
    w_slc = -(-n_slc // 128) * 128
    ov = jnp.asarray(_overlap_np(n_rows, n_cmp, n_slc, w_slc))
    t = np.arange(TQ)[:, None]
    jj = np.arange(NEAR)[None, :]
    b_near = _bias_rows(rel_bias, t - CMP_STRIDE * (jj - KC_PAD) - (CMP_LEN - 1))
    b_far = _far_bias(rel_bias, TQ)
    rows = NSA_HPG * TQ
    return pl.pallas_call(
        functools.partial(_cmp_prompt_body, n_cmp=n_cmp, n_slc=n_slc), grid=(seq // TQ,), name="cmp_prompt",
        in_specs=[pl.BlockSpec((TQ, 1024), lambda i: (i, 0)), _const_spec(kc.shape), _const_spec(vc.shape),
                  _const_spec(ov.shape), _const_spec(b_near.shape), _const_spec(b_far.shape)],
        out_specs=[pl.BlockSpec((TQ, 1024), lambda i: (i, 0)), pl.BlockSpec((NSA_GROUPS, TQ, w_slc), lambda i: (0, i, 0))],
        out_shape=[jax.ShapeDtypeStruct((seq, 1024), F32), jax.ShapeDtypeStruct((NSA_GROUPS, seq, w_slc), BF16)],
        compiler_params=_cparams(("parallel",)))(q, kc, vc, ov, b_near, b_far)


def _cmp_sample_body(q_ref, kc_ref, vc_ref, ov_ref, bias_ref, oc_ref, sel_ref, *, n_cmp, n_slc, past, t):
    q = q_ref[...]
    kc_all, vc_all = kc_ref[...].astype(BF16), vc_ref[...].astype(BF16)
    ov_all = ov_ref[...].astype(BF16)
    n_rows = kc_all.shape[0]
    rows = NSA_HPG * t
    col = lax.broadcasted_iota(jnp.int32, (rows, n_rows), 1)
    tok = lax.broadcasted_iota(jnp.int32, (rows, n_rows), 0) & (t - 1)
    j = col - KC_PAD
    mask = (j >= 0) & (j < n_cmp) & (past + tok - CMP_STRIDE * j - (CMP_LEN - 1) >= 0)
    tpos = past + lax.broadcasted_iota(jnp.int32, (t, 1), 0)
    imps = []
    for g in range(NSA_GROUPS):
        qg = _stack_heads(q, g).astype(BF16)
        s = jnp.where(mask, _dot_nt(qg, kc_all) + bias_ref[g], NEG)
        m = jnp.max(s, axis=1, keepdims=True)
        p = jnp.where(mask, jnp.exp2(s - m), 0.0)
        l = jnp.sum(p, axis=1, keepdims=True)
        p = p * (1.0 / jnp.where(l > 0.0, l, 1.0))
        o = _dot(p.astype(BF16), vc_all)
        for m4, blk in enumerate(_unstack_heads(o, g, t)):
            c0 = (g * 4 + m4) * 128
            oc_ref[:, c0:c0 + 128] = blk
        ps = p[:t]
        for r in range(1, NSA_HPG):
            ps = ps + p[r * t:(r + 1) * t]
        imps.append(_dot(ps.astype(BF16), ov_all))
    score, lane_f = _select_scores(jnp.concatenate(imps, axis=0), jnp.concatenate([tpos] * NSA_GROUPS, axis=0), n_slc)
    picked = _topk_mask(score, lane_f, min(N_SEL, n_slc))
    for g in range(NSA_GROUPS):
        sel_ref[g] = picked[g * t:(g + 1) * t].astype(sel_ref.dtype)


def _cmp_sample(q, kc, vc, rel_bias, past, t):
    db, n_rows = kc.shape[0], kc.shape[1]
    total = past + t
    n_cmp = (total - CMP_LEN) // CMP_STRIDE + 1
    n_slc = -(-total // SLC_BLOCK)
    w_slc = -(-n_slc // 128) * 128
    ov = jnp.asarray(_overlap_np(n_rows, n_cmp, n_slc, w_slc))
    tt = np.arange(t)[:, None]
    jn = np.arange(n_rows)[None, :] - KC_PAD
    bias = _bias_rows(rel_bias, past + tt - CMP_STRIDE * jn - (CMP_LEN - 1))
    return pl.pallas_call(
        functools.partial(_cmp_sample_body, n_cmp=n_cmp, n_slc=n_slc, past=past, t=t), grid=(db,), name="cmp_sample",
        in_specs=[pl.BlockSpec((t, 1024), lambda b: (b, 0)), pl.BlockSpec((None, n_rows, 128), lambda b: (b, 0, 0)),
                  pl.BlockSpec((None, n_rows, 128), lambda b: (b, 0, 0)), _const_spec(ov.shape), _const_spec(bias.shape)],
        out_specs=[pl.BlockSpec((t, 1024), lambda b: (b, 0)),
                   pl.BlockSpec((None, NSA_GROUPS, t, w_slc), lambda b: (b, 0, 0, 0))],
        out_shape=[jax.ShapeDtypeStruct((db * t, 1024), F32), jax.ShapeDtypeStruct((db, NSA_GROUPS, t, w_slc), F32)],
        compiler_params=_cparams(("parallel",)))(q, kc, vc, ov, bias)


def _sum_lane(g):
    return NSA_D * (1 - g)


def _slc_prompt_body(qi_ref, ki_ref, q_ref, ks_ref, vs_ref, sel_ref, place_ref, ep_ref, br_ref, o_ref,
                     qs_ref, m_ref, acc_ref, *, tk):
    step = pl.program_id(0)
    qi, ki = qi_ref[step], ki_ref[step]
    rows = NSA_HPG * TQ
    sub = tk // TQ

    @pl.when(ki == 0)
    def _():
        q = q_ref[...]
        for g in range(NSA_GROUPS):
            qs_ref[g] = _stack_heads(q, g).astype(BF16)
        m_ref[...] = jnp.full(m_ref.shape, NEG, F32)
        acc_ref[...] = jnp.zeros(acc_ref.shape, F32)

    near = (ki * sub + sub - 1) >= qi - 1

    def run(is_near):
        ks, vs = ks_ref[...], vs_ref[...]
        lane = lax.broadcasted_iota(jnp.int32, (tk, LANES), 1)
        for g in range(NSA_GROUPS):
            other = (lane >= NSA_D) if g == 0 else (lane < NSA_D)
            k_aug = jnp.where(other, ep_ref[g], ks)
            v_aug = jnp.where(lane == _sum_lane(g), jnp.ones((), BF16), vs)
            selpad = _dot(sel_ref[g], place_ref[g]).astype(BF16)
            q_aug = (qs_ref[g].reshape(NSA_HPG, TQ, LANES) + selpad[None]).reshape(rows, LANES)
            ts = tk
            for h in range(tk // ts):
                s = _dot_nt(q_aug, k_aug[h * ts:(h + 1) * ts])
                if is_near:
                    tok = qi * TQ - ki * tk - h * ts + lax.broadcasted_iota(jnp.int32, (TQ, ts), 0)
                    causal = jnp.where(lax.broadcasted_iota(jnp.int32, (TQ, ts), 1) <= tok, 0.0, NEG)
                    s = (s.reshape(NSA_HPG, TQ, ts) + causal[None]).reshape(rows, ts)
                    pieces = []
                    for u in range(h * ts // TQ, (h + 1) * ts // TQ):
                        delta = qi - (ki * sub + u)
                        pieces.append(jnp.where(delta == 0, br_ref[g, 0], jnp.where(delta == 1, br_ref[g, 1], 0.0)))
                    s = s + jnp.concatenate(pieces, axis=1)
                m_prev = m_ref[g]
                m_new = jnp.maximum(m_prev, jnp.max(s, axis=1, keepdims=True))
                p = jnp.exp2(s - _lane_tile(m_new, ts // LANES)).astype(BF16)
                acc_ref[g] = acc_ref[g] * jnp.exp2(m_prev - m_new) + _dot(p, v_aug[h * ts:(h + 1) * ts])
                m_ref[g] = m_new

    @pl.when(near)
    def _():
        run(True)

    @pl.when(jnp.logical_not(near))
    def _():
        run(False)

    @pl.when(ki == (qi * TQ + TQ - 1) // tk)
    def _():
        for g in range(NSA_GROUPS):
            acc = acc_ref[g]
            o = acc * (1.0 / acc[:, _sum_lane(g):_sum_lane(g) + 1])
            for m4, blk in enumerate(_unstack_heads(o, g, TQ)):
                col = (g * 4 + m4) * 128
                o_ref[:, col:col + 128] = blk


def _tile_bias(rel_bias):
    t = np.arange(TQ)[:, None]
    c = np.arange(TQ)[None, :]
    return jnp.stack([_bias_rows(rel_bias, d * TQ + t - c) for d in (0, 1)], axis=1)


def _slc_prompt(q, ksb, vsb, sel_neg, rel_bias, seq):
    tk = _key_tile(seq, 1024)
    rows = NSA_HPG * TQ
    nk = seq // tk
    bpt = tk // SLC_BLOCK
    qi, ki = _causal_schedule(seq // TQ, TQ, tk)
    b_rel = _tile_bias(rel_bias) - _far_bias(rel_bias, TQ)[:, None]
    sel_t = jnp.transpose(sel_neg[:, :, :nk * bpt].reshape(NSA_GROUPS, seq, nk, bpt), (0, 2, 1, 3))
    assert bpt <= NSA_D
    place = np.zeros((NSA_GROUPS, bpt, LANES), np.float32)
    onehot = np.zeros((NSA_GROUPS, tk, LANES), np.float32)
    for g in range(NSA_GROUPS):
        place[g, np.arange(bpt), _sum_lane(g) + np.arange(bpt)] = 1.0
        onehot[g, np.arange(tk), _sum_lane(g) + np.arange(tk) // SLC_BLOCK] = 1.0
    place, onehot = jnp.asarray(place).astype(BF16), jnp.asarray(onehot).astype(BF16)
    grid_spec = pltpu.PrefetchScalarGridSpec(
        num_scalar_prefetch=2, grid=(qi.shape[0],),
        in_specs=[pl.BlockSpec((TQ, 1024), lambda s, a, b: (a[s], 0)),
                  pl.BlockSpec((tk, 128), lambda s, a, b: (b[s], 0)),
                  pl.BlockSpec((tk, 128), lambda s, a, b: (b[s], 0)),
                  pl.BlockSpec((NSA_GROUPS, None, TQ, bpt), lambda s, a, b: (0, b[s], a[s], 0)),
                  pl.BlockSpec(place.shape, lambda s, a, b: (0, 0, 0)),
                  pl.BlockSpec(onehot.shape, lambda s, a, b: (0, 0, 0)),
                  pl.BlockSpec(b_rel.shape, lambda s, a, b: (0, 0, 0, 0))],
        out_specs=pl.BlockSpec((TQ, 1024), lambda s, a, b: (a[s], 0)),
        scratch_shapes=[pltpu.VMEM((NSA_GROUPS, rows, 128), BF16), pltpu.VMEM((NSA_GROUPS, rows, LANES), F32),
                        pltpu.VMEM((NSA_GROUPS, rows, 128), F32)])
    return pl.pallas_call(
        functools.partial(_slc_prompt_body, tk=tk), grid_spec=grid_spec, name="slc_prompt",
        out_shape=jax.ShapeDtypeStruct((seq, 1024), F32),
        compiler_params=_cparams(("arbitrary",)))(qi, ki, q, ksb, vsb, sel_t, place, onehot, b_rel)


N_WIN_BLK = WINDOW // TQ + 1


def _win_prompt_body(q_ref, *refs):
    k_refs, v_refs = refs[:N_WIN_BLK], refs[N_WIN_BLK:2 * N_WIN_BLK]
    bn_ref, bf_ref, o_ref = refs[2 * N_WIN_BLK:]
    i = pl.program_id(0)
    q = q_ref[...]
    rows = NSA_HPG * TQ
    n_keys = N_WIN_BLK * TQ
    k = jnp.concatenate([r[...] for r in k_refs], axis=0)
    v = jnp.concatenate([r[...] for r in v_refs], axis=0)
    t = lax.broadcasted_iota(jnp.int32, (rows, n_keys), 0) & (TQ - 1)
    c = lax.broadcasted_iota(jnp.int32, (rows, n_keys), 1)
    dist = WINDOW + t - c
    mask = (dist >= 0) & (dist < WINDOW) & ((i - (N_WIN_BLK - 1)) * TQ + c >= 0)
    for g in range(NSA_GROUPS):
        qg = _stack_heads(q, g).astype(BF16)
        far = jnp.broadcast_to(bf_ref[g], (rows, n_keys - 2 * TQ))
        bias = jnp.concatenate([far, bn_ref[g, 1], bn_ref[g, 0]], axis=1)
        s = jnp.where(mask, _dot_nt(qg, k) + bias, NEG)
        m = jnp.max(s, axis=1, keepdims=True)
        p = jnp.exp2(s - m)
        o = _dot(p.astype(BF16), v) / jnp.sum(p, axis=1, keepdims=True)
        for m4, blk in enumerate(_unstack_heads(o, g, TQ)):
            col = (g * 4 + m4) * 128
            o_ref[:, col:col + 128] = blk


def _win_prompt(q, kwb, vwb, rel_bias, seq):
    b_near = _tile_bias(rel_bias)
    b_far = _far_bias(rel_bias, TQ)

    def kv_spec(u):
        return pl.BlockSpec((TQ, 128), lambda i, _u=u: (jnp.maximum(i - (N_WIN_BLK - 1) + _u, 0), 0))

    return pl.pallas_call(
        _win_prompt_body, grid=(seq // TQ,), name="win_prompt",
        in_specs=[pl.BlockSpec((TQ, 1024), lambda i: (i, 0))] + [kv_spec(u) for u in range(N_WIN_BLK)] * 2
        + [_const_spec(b_near.shape), _const_spec(b_far.shape)],
        out_specs=pl.BlockSpec((TQ, 1024), lambda i: (i, 0)),
        out_shape=jax.ShapeDtypeStruct((seq, 1024), F32),
        compiler_params=_cparams(("parallel",)))(q, *([kwb] * N_WIN_BLK), *([vwb] * N_WIN_BLK), b_near, b_far)


def _slc_sample_body(pt_ref, q_ref, sel_ref, seln_ref, ex_ref, bias_ref, bnew_ref, kn_ref, vn_ref, k_hbm, v_hbm,
                     o_ref, kbuf, vbuf, sems, qs_ref, m_ref, l_ref, acc_ref, *, pp, t, li):
    slot = _paged_pipeline(pt_ref, (k_hbm, v_hbm), (kbuf, vbuf), (sems.at[0], sems.at[1]), li, pp)
    pc = pl.program_id(1)
    rows = NSA_HPG * t
    tk = pp * PAGE

    @pl.when(pc == 0)
    def _():
        q = q_ref[...]
        for g in range(NSA_GROUPS):
            qs_ref[g] = _stack_heads(q, g).astype(BF16)
            _flash_init(m_ref.at[g], l_ref.at[g], acc_ref.at[g])

    feat = NSA_GROUPS * NSA_D
    ks = [kbuf[slot, j].reshape(feat, PAGE).astype(BF16) for j in range(pp)]
    vs = [vbuf[slot, j].reshape(feat, PAGE).astype(BF16) for j in range(pp)]
    ex = ex_ref[...]
    for g in range(NSA_GROUPS):
        qg = qs_ref[g]
        s = jnp.concatenate([_dot(qg, ks[j]) for j in range(pp)], axis=1) + bias_ref[g]
        selx = _dot(sel_ref[g].astype(BF16), ex)
        mask = jnp.concatenate([selx] * NSA_HPG, axis=0) > 0.5

        def pv(p):
            acc = _dot_nt(p[:, :PAGE], vs[0])
            for j in range(1, pp):
                acc = acc + _dot_nt(p[:, j * PAGE:(j + 1) * PAGE], vs[j])
            return acc

        _online_update(s, mask, pv, m_ref.at[g], l_ref.at[g], acc_ref.at[g])

    @pl.when(pc == pl.num_programs(1) - 1)
    def _():
        kn, vn = _pad_rows16(kn_ref[...]).astype(BF16), _pad_rows16(vn_ref[...]).astype(BF16)
        tok = lax.broadcasted_iota(jnp.int32, (rows, NEW_PAD), 0) & (t - 1)
        key = lax.broadcasted_iota(jnp.int32, (rows, NEW_PAD), 1)
        for g in range(NSA_GROUPS):
            s = _dot_nt(qs_ref[g], kn) + bnew_ref[g]
            seln = jnp.concatenate([seln_ref[g]] * NSA_HPG, axis=0) > 0.5
            _online_update(s, seln & (key <= tok), lambda p: _dot(p, vn), m_ref.at[g], l_ref.at[g], acc_ref.at[g])
            o = _flash_result(l_ref.at[g], acc_ref.at[g])
            for m4, blk in enumerate(_unstack_heads(o, g, t)):
                col = (g * 4 + m4) * 128
                o_ref[:, col:col + 128] = blk


def _slc_sample(q, sel, ks_new, vs_new, pool_k, pool_v, li, page_table, rel_bias, t):
    db, n_pages = page_table.shape
    past = n_pages * PAGE
    pp = _pages_per_step(n_pages)
    n_steps = n_pages // pp
    tk = pp * PAGE
    bpt = tk // SLC_BLOCK
    rows = NSA_HPG * t
    sel_past = sel[..., :n_steps * bpt].reshape(db, NSA_GROUPS, t, n_steps, bpt)
    sel_past = jnp.transpose(sel_past, (0, 3, 1, 2, 4))
    new_blk = past // SLC_BLOCK
    sel_new = jnp.broadcast_to(sel[..., new_blk:new_blk + 1], (db, NSA_GROUPS, t, NEW_PAD))
    expand = jnp.asarray((np.arange(bpt)[:, None] == np.arange(tk)[None, :] // SLC_BLOCK).astype(np.float32)).astype(BF16)
    tt = np.arange(t)[:, None]
    assert tk >= FAR_DIST
    last_keys = (n_steps - 1) * tk + np.arange(tk)[None, :]
    bias = jnp.stack([jnp.broadcast_to(_far_bias(rel_bias, t), (NSA_GROUPS, rows, tk)),
                      _bias_rows(rel_bias, past + tt - last_keys)])
    b_new = _bias_rows(rel_bias, tt - np.arange(NEW_PAD)[None, :])
    hbm_spec = pl.BlockSpec(memory_space=pl.ANY)
    page_buf = pltpu.VMEM((2, pp, NSA_GROUPS, NSA_D, PAGE), F32)
    grid_spec = pltpu.PrefetchScalarGridSpec(
        num_scalar_prefetch=1, grid=(db, n_steps),
        in_specs=[pl.BlockSpec((t, 1024), lambda b, c, pt: (b, 0)),
                  pl.BlockSpec((None, None, NSA_GROUPS, t, bpt), lambda b, c, pt: (b, c, 0, 0, 0)),
                  pl.BlockSpec((None, NSA_GROUPS, t, NEW_PAD), lambda b, c, pt: (b, 0, 0, 0)),
                  pl.BlockSpec(expand.shape, lambda b, c, pt: (0, 0)),
                  pl.BlockSpec((None, NSA_GROUPS, rows, tk),
                               lambda b, c, pt: (jnp.where(c == n_steps - 1, 1, 0), 0, 0, 0)),
                  pl.BlockSpec(b_new.shape, lambda b, c, pt: (0, 0, 0)),
                  pl.BlockSpec((t, 128), lambda b, c, pt: (b, 0)),
                  pl.BlockSpec((t, 128), lambda b, c, pt: (b, 0)), hbm_spec, hbm_spec],
        out_specs=pl.BlockSpec((t, 1024), lambda b, c, pt: (b, 0)),
        scratch_shapes=[page_buf, page_buf, pltpu.SemaphoreType.DMA((2, 2)),
                        pltpu.VMEM((NSA_GROUPS, rows, 128), BF16), pltpu.VMEM((NSA_GROUPS, rows, 1), F32),
                        pltpu.VMEM((NSA_GROUPS, rows, 1), F32), pltpu.VMEM((NSA_GROUPS, rows, 128), F32)])
    return pl.pallas_call(
        functools.partial(_slc_sample_body, pp=pp, t=t, li=li), grid_spec=grid_spec, name="slc_sample",
        out_shape=jax.ShapeDtypeStruct((db * t, 1024), F32),
        compiler_params=_cparams(("arbitrary", "arbitrary")))(
            page_table, q, sel_past, sel_new, expand, bias, b_new, ks_new, vs_new,
            _feature_major(pool_k), _feature_major(pool_v))


def _win_sample_body(q_ref, wk_ref, wv_ref, kn_ref, vn_ref, bias_ref, o_ref, *, t, w_buf):
    q = q_ref[...]
    rows = NSA_HPG * t
    feat = NSA_GROUPS * NSA_D
    kt = wk_ref[...].reshape(feat, w_buf).astype(BF16)
    vt = wv_ref[...].reshape(feat, w_buf).astype(BF16)
    kn, vn = _pad_rows16(kn_ref[...]).astype(BF16), _pad_rows16(vn_ref[...]).astype(BF16)
    n_keys = w_buf + NEW_PAD
    tok = lax.broadcasted_iota(jnp.int32, (rows, n_keys), 0) & (t - 1)
    c = lax.broadcasted_iota(jnp.int32, (rows, n_keys), 1)
    dist = w_buf + tok - c
    mask = (dist >= 0) & (dist < WINDOW)
    for g in range(NSA_GROUPS):
        qg = _stack_heads(q, g).astype(BF16)
        s = jnp.concatenate([_dot(qg, kt), _dot_nt(qg, kn)], axis=1) + bias_ref[g]
        s = jnp.where(mask, s, NEG)
        m = jnp.max(s, axis=1, keepdims=True)
        p = jnp.exp2(s - m)
        pb = p.astype(BF16)
        o = (_dot_nt(pb[:, :w_buf], vt) + _dot(pb[:, w_buf:], vn)) / jnp.sum(p, axis=1, keepdims=True)
        for m4, blk in enumerate(_unstack_heads(o, g, t)):
            col = (g * 4 + m4) * 128
            o_ref[:, col:col + 128] = blk


def _win_sample(q, wk_buf, wv_buf, kw_new, vw_new, rel_bias, t):
    db, w_buf = wk_buf.shape[0], wk_buf.shape[1]
    tt = np.arange(t)[:, None]
    bias = _bias_rows(rel_bias, w_buf + tt - np.arange(w_buf + NEW_PAD)[None, :])
    wk = jnp.transpose(wk_buf, (0, 2, 3, 1))
    wv = jnp.transpose(wv_buf, (0, 2, 3, 1))
    buf_spec = pl.BlockSpec((None, NSA_GROUPS, NSA_D, w_buf), lambda b: (b, 0, 0, 0))
    return pl.pallas_call(
        functools.partial(_win_sample_body, t=t, w_buf=w_buf), grid=(db,), name="win_sample",
        in_specs=[pl.BlockSpec((t, 1024), lambda b: (b, 0)), buf_spec, buf_spec, pl.BlockSpec((t, 128), lambda b: (b, 0)),
                  pl.BlockSpec((t, 128), lambda b: (b, 0)), _const_spec(bias.shape)],
        out_specs=pl.BlockSpec((t, 1024), lambda b: (b, 0)),
        out_shape=jax.ShapeDtypeStruct((db * t, 1024), F32),
        compiler_params=_cparams(("parallel",)))(q, wk, wv, kw_new, vw_new, bias)


def kernel(x_prompt, x_sample, cache_mla_latent, cache_mla_krope, state_ssm, state_conv, cache_cmp_k, cache_cmp_v, cache_slc_k, cache_slc_v, state_win_k, state_win_v, page_table, rel_bias, w_in_a, q_norm, w_uq, kv_norm, w_uk, w_uv, conv_w, conv_b, dt_bias, a_log, d_skip, ssm_norm, w_out_a, ln_a_g, ln_a_b, w_in_c, cmp_pe_k, cmp_w1_k, cmp_w2_k, cmp_pe_v, cmp_w1_v, cmp_w2_v, w_out_c, ln_c_g, ln_c_b):
    bp, sp = x_prompt.shape[:2]
    db, ss = x_sample.shape[:2]
    n_pages = page_table.shape[1]
    past = n_pages * PAGE
    assert bp == 1 and x_prompt.shape[2] == D_MODEL and cache_mla_latent.shape[2] == PAGE
    assert sp % TQ == 0 and sp >= WINDOW and ss == 8
    assert past % SLC_BLOCK == 0 and ss < CMP_STRIDE and past % CMP_STRIDE == 0 and past >= WINDOW
    assert state_win_k.shape[2] == WINDOW

    hp = x_prompt.reshape(sp, D_MODEL)
    hs = x_sample.reshape(db * ss, D_MODEL)
    pos_p = jnp.arange(sp)
    pos_s = jnp.tile(past + jnp.arange(ss), db)
    st = {}

    pa = _prep_a(w_in_a[0], q_norm[0], w_uq[0], kv_norm[0], w_uk[0], w_uv[0], dt_bias[0], w_out_a[0])
    qlat, qrope, lat, latb, kr, krb, g, z, xbc, dt = _a_front(hp, pos_p, pa)
    o_lat = _mla_prompt(qlat.reshape(sp * MLA_HEADS, KV_RANK), qrope.reshape(sp * MLA_HEADS, ROPE), latb, krb)
    y, cbuf, hst = _ssd(xbc, dt, jnp.zeros((1, CONV_W - 1, CONV_DIM), F32),
                        jnp.zeros((1, SSM_HEADS, SSM_HEADDIM, D_STATE), F32), conv_w[0], conv_b[0], a_log[0],
                        d_skip[0], 1)
    hp = _a_back(hp, o_lat.reshape(sp, MLA_HEADS * KV_RANK), g, y, z, pa, ssm_norm[0], ln_a_g[0], ln_a_b[0])
    st['p_lat'], st['p_krope'] = lat.reshape(1, 1, sp, KV_RANK), kr.reshape(1, 1, sp, ROPE)
    st['p_ssm'], st['p_conv'] = hst[None], cbuf[None]
    qlat, qrope, lat, latb, kr, krb, g, z, xbc, dt = _a_front(hs, pos_s, pa)
    o_lat = _mla_sample(qlat.reshape(db * ss * MLA_HEADS, KV_RANK), qrope.reshape(db * ss * MLA_HEADS, ROPE), lat, kr,
                        cache_mla_latent, cache_mla_krope, 0, page_table, ss)
    y, cbuf, hst = _ssd(xbc, dt, state_conv[0], state_ssm[0], conv_w[0], conv_b[0], a_log[0], d_skip[0], db)
    hs = _a_back(hs, o_lat.reshape(db * ss, MLA_HEADS * KV_RANK), g, y, z, pa, ssm_norm[0], ln_a_g[0], ln_a_b[0])
    st['s_lat'], st['s_krope'] = lat.reshape(1, db, ss, KV_RANK), kr.reshape(1, db, ss, ROPE)
    st['s_ssm'], st['s_conv'] = hst[None], cbuf[None]

    pc = _prep_c(w_in_c[0], w_out_c[0])
    pk = _prep_cmp(cmp_pe_k[0], cmp_w1_k[0], cmp_w2_k[0])
    pv = _prep_cmp(cmp_pe_v[0], cmp_w1_v[0], cmp_w2_v[0])
    kv5 = lambda a, b_, s_: a.reshape(1, b_, s_, NSA_GROUPS, NSA_D)
    q, kc_r, vc_r, ks, vs, kw, vw, ksb, vsb, kwb, vwb, og = _c_front(hp, pc)
    kc, vc = _cmp_finish(_cmp_u_prompt(kc_r, pk[0]), _cmp_u_prompt(vc_r, pv[0]), pk, pv, 1)
    o_c, sel = _cmp_prompt(q, kc[0], vc[0], rel_bias, sp)
    o_s = _slc_prompt(q, ksb, vsb, sel, rel_bias, sp)
    o_w = _win_prompt(q, kwb, vwb, rel_bias, sp)
    hp = _c_back(hp, o_c, o_s, o_w, og, pc, ln_c_g[0], ln_c_b[0])
    keep = min(WINDOW, sp)
    for n, a in (('p_cmp_k', kc_r), ('p_cmp_v', vc_r), ('p_slc_k', ks), ('p_slc_v', vs)):
        st[n] = kv5(a, 1, sp)
    st['p_win_k'], st['p_win_v'] = kv5(kw[sp - keep:], 1, keep), kv5(vw[sp - keep:], 1, keep)
    q, kc_r, vc_r, ks, vs, kw, vw, ksb, vsb, kwb, vwb, og = _c_front(hs, pc)
    uk, uv = _cmp_u_paged(cache_cmp_k, cache_cmp_v, 0, page_table, pk[0], pv[0])
    n_chunks = uk.shape[1]
    kc, vc = _cmp_finish(uk.reshape(db * n_chunks, -1), uv.reshape(db * n_chunks, -1), pk, pv, db)
    o_c, sel = _cmp_sample(q, kc, vc, rel_bias, past, ss)
    o_s = _slc_sample(q, sel, ks, vs, cache_slc_k, cache_slc_v, 0, page_table, rel_bias, ss)
    o_w = _win_sample(q, state_win_k[0], state_win_v[0], kw, vw, rel_bias, ss)
    hs = _c_back(hs, o_c, o_s, o_w, og, pc, ln_c_g[0], ln_c_b[0])
    for n, a in (('s_cmp_k', kc_r), ('s_cmp_v', vc_r), ('s_slc_k', ks), ('s_slc_v', vs)):
        st[n] = kv5(a, db, ss)
    new5 = lambda a: a.reshape(db, ss, NSA_GROUPS, NSA_D)
    st['s_win_k'] = jnp.concatenate([state_win_k[0], new5(kw)], 1)[None, :, ss:]
    st['s_win_v'] = jnp.concatenate([state_win_v[0], new5(vw)], 1)[None, :, ss:]

    names = ['p_lat', 'p_krope', 'p_ssm', 'p_conv', 'p_cmp_k', 'p_cmp_v', 'p_slc_k', 'p_slc_v', 'p_win_k', 'p_win_v',
             's_lat', 's_krope', 's_ssm', 's_conv', 's_cmp_k', 's_cmp_v', 's_slc_k', 's_slc_v', 's_win_k', 's_win_v']
    return (hp.reshape(bp, sp, D_MODEL), hs.reshape(db, ss, D_MODEL)) + tuple(st[n] for n in names)
```
